```python
import math
import jax, jax.numpy as jnp
from jax import lax
import numpy as np

D_MODEL = 4096
BATCH = 2
SEQ = 8192
DEPTH = 4

N_MIXERS = 3
MIX_WIDTH = D_MODEL
EPS = 1e-6
GRID_W = 64
NA_HEAD_DIM = 128
NA_HEADS = MIX_WIDTH // NA_HEAD_DIM
NA_ROWS_MAX = 8
NA_COLS = 16
S5_GROUP = 16
S5_GROUPS = MIX_WIDTH // S5_GROUP
S5_STATE = 64
S5_DT_MIN = 1e-3
S5_DT_MAX = 1e-1
GLA_DV = 512
GLA_HEADS = MIX_WIDTH // GLA_DV
GLA_KEY_WIDTH = MIX_WIDTH // 2
GLA_DK = GLA_KEY_WIDTH // GLA_HEADS
GLA_GATE_RANK = 16
GLA_GATE_NORM = 16.0
GLA_CHUNK = 64
N_NA_LAYERS = (DEPTH + 2) // 3
N_S5_LAYERS = (DEPTH + 1) // 3
N_GLA_LAYERS = DEPTH // 3

kernel_name = 'bidir_hybrid_na_s5_gla_adaln'


def _rmsnorm(x, w):
    xf = x.astype(jnp.float32)
    y = xf * lax.rsqrt(jnp.mean(xf * xf, axis=-1, keepdims=True) + EPS) * w.astype(jnp.float32)
    return y.astype(x.dtype)


def _na_mixer(h, w_in, rpb, w_out):
    bsz, seq, _ = h.shape
    q, k, v, g = jnp.split(h @ w_in, 4, axis=-1)
    rows = seq // GRID_W
    kr = min(NA_ROWS_MAX, rows)
    grid = lambda t: t.reshape(bsz, rows, GRID_W, NA_HEADS, NA_HEAD_DIM)
    qg, kg, vg = grid(q), grid(k), grid(v)
    col = jnp.arange(GRID_W)
    col_start = jnp.clip(col - NA_COLS // 2, 0, GRID_W - NA_COLS)
    col_idx = col_start[:, None] + jnp.arange(NA_COLS)[None, :]
    bias_cols = rpb.astype(jnp.float32)[:, :, col_idx - col[:, None] + NA_COLS - 1]
    scale = NA_HEAD_DIM ** -0.5

    def row_block(r):
        row_start = jnp.clip(r - kr // 2, 0, rows - kr)
        q_r = lax.dynamic_index_in_dim(qg, r, axis=1, keepdims=False)
        k_win = lax.dynamic_slice_in_dim(kg, row_start, kr, axis=1)[:, :, col_idx]
        v_win = lax.dynamic_slice_in_dim(vg, row_start, kr, axis=1)[:, :, col_idx]
        dr = row_start + jnp.arange(kr) - r + NA_ROWS_MAX - 1
        bias = jnp.take(bias_cols, dr, axis=1).transpose(0, 2, 1, 3)
        s = jnp.einsum('bqhd,bkqjhd->bhqkj', q_r, k_win).astype(jnp.float32) * scale + bias[None]
        p = jax.nn.softmax(s, axis=(-2, -1)).astype(v_win.dtype)
        return jnp.einsum('bhqkj,bkqjhd->bqhd', p, v_win)

    out = lax.map(row_block, jnp.arange(rows))
    out = out.transpose(1, 0, 2, 3, 4).reshape(bsz, seq, MIX_WIDTH)
    return (out * jax.nn.silu(g)) @ w_out


def _s5_states(u, a_re, a_im, log_step, b_re, b_im):
    dt = jnp.exp(log_step.astype(jnp.float32))
    a_re = a_re.astype(jnp.float32)
    a_im = a_im.astype(jnp.float32)
    mag = jnp.exp(a_re * dt)
    lam_re = mag * jnp.cos(a_im * dt)
    lam_im = mag * jnp.sin(a_im * dt)
    den = a_re * a_re + a_im * a_im
    nr = lam_re - 1.0
    r_re = ((nr * a_re + lam_im * a_im) / den)[..., None]
    r_im = ((lam_im * a_re - nr * a_im) / den)[..., None]
    b_re = b_re.astype(jnp.float32)
    b_im = b_im.astype(jnp.float32)
    bb_re = r_re * b_re - r_im * b_im
    bb_im = r_re * b_im + r_im * b_re
    bu_re = jnp.einsum('blgi,gpi->lbgp', u, bb_re)
    bu_im = jnp.einsum('blgi,gpi->lbgp', u, bb_im)
    seq = u.shape[1]
    lam_re_l = jnp.broadcast_to(lam_re[None, None], (seq, 1) + lam_re.shape)
    lam_im_l = jnp.broadcast_to(lam_im[None, None], (seq, 1) + lam_im.shape)

    def combine(e1, e2):
        a1r, a1i, b1r, b1i = e1
        a2r, a2i, b2r, b2i = e2
        return (a2r * a1r - a2i * a1i,
                a2r * a1i + a2i * a1r,
                a2r * b1r - a2i * b1i + b2r,
                a2r * b1i + a2i * b1r + b2i)

    _, _, x_re, x_im = lax.associative_scan(combine, (lam_re_l, lam_im_l, bu_re, bu_im), axis=0)
    return x_re, x_im


def _s5_mixer(h, w_in, b_re, b_im, c_re, c_im, d_skip, a_re_f, a_im_f, a_re_b, a_im_b,
              log_step_f, log_step_b, w_glu, b_glu, w_out):
    bsz, seq, _ = h.shape
    u, g = jnp.split(h @ w_in, 2, axis=-1)
    uf = u.reshape(bsz, seq, S5_GROUPS, S5_GROUP).astype(jnp.float32)
    xr_f, xi_f = _s5_states(uf, a_re_f, a_im_f, log_step_f, b_re, b_im)
    xr_b, xi_b = _s5_states(jnp.flip(uf, axis=1), a_re_b, a_im_b, log_step_b, b_re, b_im)
    x_re = xr_f + jnp.flip(xr_b, axis=0)
    x_im = xi_f + jnp.flip(xi_b, axis=0)
    y = (jnp.einsum('lbgp,gip->blgi', x_re, c_re.astype(jnp.float32))
         - jnp.einsum('lbgp,gip->blgi', x_im, c_im.astype(jnp.float32))
         + d_skip.astype(jnp.float32) * uf)
    y = jax.nn.gelu(y.reshape(bsz, seq, MIX_WIDTH).astype(h.dtype))
    y = y * jax.nn.sigmoid(y @ w_glu + b_glu)
    return (y * jax.nn.silu(g)) @ w_out


def _gla_chunked(q, k, v, gk, include_diag):
    bsz, seq, heads, dk = q.shape
    dv = v.shape[-1]
    n = seq // GLA_CHUNK
    chunks = lambda t: t.reshape(bsz, n, GLA_CHUNK, heads, t.shape[-1])
    q, k, v, gk = chunks(q), chunks(k), chunks(v), chunks(gk)
    b = jnp.cumsum(gk, axis=2)
    b_last = b[:, :, -1:]
    q_s = q * jnp.exp(b)
    k_s = k * jnp.exp(-b)
    idx = jnp.arange(GLA_CHUNK)
    mask = (idx[:, None] >= idx[None, :]) if include_diag else (idx[:, None] > idx[None, :])
    scores = jnp.einsum('bnihd,bnjhd->bnhij', q_s, k_s)
    scores = jnp.where(mask, scores, 0.0)
    o_intra = jnp.einsum('bnhij,bnjhv->bnihv', scores, v)
    k_end = k * jnp.exp(b_last - b)
    decay = jnp.exp(b_last[:, :, 0])

    def step(state, xs):
        q_n, k_n, v_n, d_n = xs
        o_n = jnp.einsum('bihd,bhdv->bihv', q_n, state)
        state = d_n[..., None] * state + jnp.einsum('bjhd,bjhv->bhdv', k_n, v_n)
        return state, o_n

    s0 = jnp.zeros((bsz, heads, dk, dv), jnp.float32)
    _, o_inter = lax.scan(step, s0, (q_s.swapaxes(0, 1), k_end.swapaxes(0, 1),
                                     v.swapaxes(0, 1), decay.swapaxes(0, 1)))
    o = o_intra + o_inter.swapaxes(0, 1)
    return o.reshape(bsz, seq, heads, dv)


def _gla_mixer(h, w_in, w1_f, w2_f, bg_f, w1_b, w2_b, bg_b, norm_w, w_out):
    bsz, seq, _ = h.shape
    q, k, v, g = jnp.split(h @ w_in, [GLA_KEY_WIDTH, 2 * GLA_KEY_WIDTH, 2 * GLA_KEY_WIDTH + MIX_WIDTH], axis=-1)
    heads = lambda t, d: t.reshape(bsz, seq, GLA_HEADS, d).astype(jnp.float32)
    q = heads(q, GLA_DK) * (GLA_DK ** -0.5)
    k = heads(k, GLA_DK)
    v = heads(v, GLA_DV)
    gate = lambda w1, w2, bg: heads(jax.nn.log_sigmoid(((h @ w1) @ w2 + bg).astype(jnp.float32)) / GLA_GATE_NORM, GLA_DK)
    gk_f = gate(w1_f, w2_f, bg_f)
    gk_b = gate(w1_b, w2_b, bg_b)
    flip = lambda t: jnp.flip(t, axis=1)
    o = (_gla_chunked(q, k, v, gk_f, True)
         + flip(_gla_chunked(flip(q), flip(k), flip(v), flip(gk_b), False)))
    o = o * lax.rsqrt(jnp.mean(o * o, axis=-1, keepdims=True) + EPS) * norm_w.astype(jnp.float32)
    o = o.reshape(bsz, seq, MIX_WIDTH).astype(h.dtype)
    return (o * jax.nn.silu(g)) @ w_out


def setup_inputs(seed: int = 0) -> dict:
    key = jax.random.key(seed)
    keys = iter(jax.random.split(key, 40))
    nrm = lambda shape, s: jax.random.normal(next(keys), shape, jnp.float32) * s
    E, D = MIX_WIDTH, D_MODEL
    G, P, I = S5_GROUPS, S5_STATE, S5_GROUP
    a_im_base = math.pi * jnp.arange(P, dtype=jnp.float32)
    log_step = lambda: jax.random.uniform(next(keys), (N_S5_LAYERS, G, P), jnp.float32,
                                          math.log(S5_DT_MIN), math.log(S5_DT_MAX))
    return {
        'x': nrm((BATCH, SEQ, D), 1.0),
        'c': nrm((BATCH, D), 1.0),
        'mod_w': nrm((DEPTH, D, 3 * D), 0.5 * D ** -0.5),
        'mod_b': nrm((DEPTH, 3 * D), 0.02),
        'norm_w': 1.0 + nrm((DEPTH, D), 0.02),
        'na_w_in': nrm((N_NA_LAYERS, D, 4 * E), D ** -0.5),
        'na_rpb': nrm((N_NA_LAYERS, NA_HEADS, 2 * NA_ROWS_MAX - 1, 2 * NA_COLS - 1), 0.1),
        'na_w_out': nrm((N_NA_LAYERS, E, D), E ** -0.5),
        's5_w_in': nrm((N_S5_LAYERS, D, 2 * E), D ** -0.5),
        's5_b_re': nrm((N_S5_LAYERS, G, P, I), (2 * I) ** -0.5),
        's5_b_im': nrm((N_S5_LAYERS, G, P, I), (2 * I) ** -0.5),
        's5_c_re': nrm((N_S5_LAYERS, G, I, P), (2 * P) ** -0.5),
        's5_c_im': nrm((N_S5_LAYERS, G, I, P), (2 * P) ** -0.5),
        's5_d': nrm((N_S5_LAYERS, G, I), 1.0),
        's5_a_re_fwd': -0.5 + nrm((N_S5_LAYERS, G, P), 0.01),
        's5_a_im_fwd': a_im_base + nrm((N_S5_LAYERS, G, P), 0.01),
        's5_a_re_bwd': -0.5 + nrm((N_S5_LAYERS, G, P), 0.01),
        's5_a_im_bwd': a_im_base + nrm((N_S5_LAYERS, G, P), 0.01),
        's5_log_step_fwd': log_step(),
        's5_log_step_bwd': log_step(),
        's5_w_glu': nrm((N_S5_LAYERS, E, E), E ** -0.5),
        's5_b_glu': nrm((N_S5_LAYERS, E), 0.02),
        's5_w_out': nrm((N_S5_LAYERS, E, D), E ** -0.5),
        'gla_w_in': nrm((N_GLA_LAYERS, D, 2 * GLA_KEY_WIDTH + 2 * E), D ** -0.5),
        'gla_gk_w1_fwd': nrm((N_GLA_LAYERS, D, GLA_GATE_RANK), D ** -0.5),
        'gla_gk_w2_fwd': nrm((N_GLA_LAYERS, GLA_GATE_RANK, GLA_KEY_WIDTH), GLA_GATE_RANK ** -0.5),
        'gla_gk_b_fwd': nrm((N_GLA_LAYERS, GLA_KEY_WIDTH), 0.1),
        'gla_gk_w1_bwd': nrm((N_GLA_LAYERS, D, GLA_GATE_RANK), D ** -0.5),
        'gla_gk_w2_bwd': nrm((N_GLA_LAYERS, GLA_GATE_RANK, GLA_KEY_WIDTH), GLA_GATE_RANK ** -0.5),
        'gla_gk_b_bwd': nrm((N_GLA_LAYERS, GLA_KEY_WIDTH), 0.1),
        'gla_norm_w': 1.0 + nrm((N_GLA_LAYERS, GLA_DV), 0.02),
        'gla_w_out': nrm((N_GLA_LAYERS, E, D), E ** -0.5),
        'final_norm_w': 1.0 + nrm((D,), 0.02),
    }


def reference(x, c, mod_w, mod_b, norm_w, na_w_in, na_rpb, na_w_out,
              s5_w_in, s5_b_re, s5_b_im, s5_c_re, s5_c_im, s5_d,
              s5_a_re_fwd, s5_a_im_fwd, s5_a_re_bwd, s5_a_im_bwd,
              s5_log_step_fwd, s5_log_step_bwd, s5_w_glu, s5_b_glu, s5_w_out,
              gla_w_in, gla_gk_w1_fwd, gla_gk_w2_fwd, gla_gk_b_fwd,
              gla_gk_w1_bwd, gla_gk_w2_bwd, gla_gk_b_bwd, gla_norm_w, gla_w_out,
              final_norm_w):
    c_act = jax.nn.silu(c)
    for i in range(DEPTH):
        kind, j = i % N_MIXERS, i // N_MIXERS
        mod = c_act @ mod_w[i] + mod_b[i]
        shift, scale, gate = [m[:, None, :] for m in jnp.split(mod, 3, axis=-1)]
        h = _rmsnorm(x, norm_w[i]) * (1.0 + scale) + shift
        if kind == 0:
            y = _na_mixer(h, na_w_in[j], na_rpb[j], na_w_out[j])
        elif kind == 1:
            y = _s5_mixer(h, s5_w_in[j], s5_b_re[j], s5_b_im[j], s5_c_re[j], s5_c_im[j], s5_d[j],
                          s5_a_re_fwd[j], s5_a_im_fwd[j], s5_a_re_bwd[j], s5_a_im_bwd[j],
                          s5_log_step_fwd[j], s5_log_step_bwd[j], s5_w_glu[j], s5_b_glu[j], s5_w_out[j])
        else:
            y = _gla_mixer(h, gla_w_in[j], gla_gk_w1_fwd[j], gla_gk_w2_fwd[j], gla_gk_b_fwd[j],
                           gla_gk_w1_bwd[j], gla_gk_w2_bwd[j], gla_gk_b_bwd[j], gla_norm_w[j], gla_w_out[j])
        x = x + gate * y
    return _rmsnorm(x, final_norm_w)
```

```python
import functools
import math

import jax
import jax.numpy as jnp
from jax import lax
from jax.experimental import pallas as pl
from jax.experimental.pallas import tpu as pltpu

EPS = 1e-6
GRID_W = 64
NA_HEAD_DIM = 128
NA_ROWS = 8
NA_COLS = 16
NA_ROW_BLOCK = 8
S5_GROUP = 16
S5_STATE = 64
GLA_DV = 512
GLA_DK = 256
GLA_GATE_RANK = 16
GLA_GATE_NORM = 16.0
GLA_CHUNK = 64
NEG_INF = -1e30

V7X_VMEM_BYTES = 64 * 1024 * 1024
VMEM_LIMIT = V7X_VMEM_BYTES - 8 * 1024 * 1024
LANES = 128
BF16 = jnp.bfloat16
F32 = jnp.float32


def _params(*sem):
    return pltpu.CompilerParams(dimension_semantics=sem, vmem_limit_bytes=VMEM_LIMIT)


def _silu(x):
    return x * jax.nn.sigmoid(x)


def _log_sigmoid(x):
    return jnp.minimum(x, 0.0) - jnp.log1p(jnp.exp(-jnp.abs(x)))


def _dot(a, b):
    return jnp.dot(a, b, preferred_element_type=F32)


def _dot_nt(a, b):
    return lax.dot_general(a, b, (((1,), (1,)), ((), ())), preferred_element_type=F32)


def _dot_tn(a, b):
    return lax.dot_general(a, b, (((0,), (0,)), ((), ())), preferred_element_type=F32)


def _fused_matmul(name, a_ins, a_specs, prologue, w, e_ins, e_specs, epilogue,
                  m, k, n, tm, tn, out_dtype):
    n_a, n_e = len(a_ins), len(e_ins)

    def body(*refs):
        a_refs = refs[:n_a]
        w_ref = refs[n_a]
        e_refs = refs[n_a + 1:n_a + 1 + n_e]
        o_ref = refs[n_a + 1 + n_e]
        if prologue is None:
            lhs = a_refs[0][...]
        else:
            h_ref = refs[n_a + 2 + n_e]

            @pl.when(pl.program_id(1) == 0)
            def _():
                prologue(h_ref, *a_refs)

            lhs = h_ref[...]
        acc = _dot(lhs, w_ref[...])
        o_ref[...] = epilogue(acc, *e_refs).astype(out_dtype)

    scratch = [] if prologue is None else [pltpu.VMEM((tm, k), BF16)]
    return pl.pallas_call(
        body,
        grid=(m // tm, n // tn),
        in_specs=list(a_specs) + [pl.BlockSpec((k, tn), lambda i, j: (0, j))] + list(e_specs),
        out_specs=pl.BlockSpec((tm, tn), lambda i, j: (i, j)),
        out_shape=jax.ShapeDtypeStruct((m, n), out_dtype),
        scratch_shapes=scratch,
        compiler_params=_params("parallel", "arbitrary"),
        name=name,
    )(*a_ins, w, *e_ins)


def _row_tile(seq, want):
    tm = min(want, seq)
    assert seq % tm == 0
    return tm


def _col_tile(n, want):
    tn = min(want, n)
    assert n % tn == 0
    return tn


def _modulation(c, mod_w, mod_b):
    depth, d, n = mod_w.shape
    bsz = c.shape[0]
    rows = 8
    c_pad = jnp.zeros((rows, d), F32).at[:bsz].set(c)
    tn = _col_tile(n, 512)

    def body(c_ref, w_ref, b_ref, o_ref):
        act = _silu(c_ref[...]).astype(BF16)
        o_ref[0] = _dot(act, w_ref[0].astype(BF16)) + b_ref[0]

    out = pl.pallas_call(
        body,
        grid=(depth, n // tn),
        in_specs=[pl.BlockSpec((rows, d), lambda i, j: (0, 0)),
                  pl.BlockSpec((1, d, tn), lambda i, j: (i, 0, j)),
                  pl.BlockSpec((1, 1, tn), lambda i, j: (i, 0, j))],
        out_specs=pl.BlockSpec((1, rows, tn), lambda i, j: (i, 0, j)),
        out_shape=jax.ShapeDtypeStruct((depth, rows, n), F32),
        compiler_params=_params("parallel", "parallel"),
        name="modulation",
    )(c_pad, mod_w, mod_b.reshape(depth, 1, n))
    return out[:, :bsz]


def _in_proj(name, x2, seq, norm_w, scale, shift, w, out_dtype, tm=512, tn=1024):
    m, d = x2.shape
    n = w.shape[1]
    tm = _row_tile(seq, tm)
    tn = _col_tile(n, tn)
    tiles_per_seq = seq // tm
    bsz = m // seq

    def prologue(h_ref, x_ref, nw_ref, sc_ref, sh_ref):
        xv = x_ref[...]
        y = xv * lax.rsqrt(jnp.mean(xv * xv, axis=-1, keepdims=True) + EPS) * nw_ref[...]
        h_ref[...] = (y * (1.0 + sc_ref[0]) + sh_ref[0]).astype(BF16)

    vec_spec = pl.BlockSpec((1, 1, d), lambda i, j: (i // tiles_per_seq, 0, 0))
    return _fused_matmul(
        name,
        [x2, norm_w.reshape(1, d), scale.reshape(bsz, 1, d), shift.reshape(bsz, 1, d)],
        [pl.BlockSpec((tm, d), lambda i, j: (i, 0)),
         pl.BlockSpec((1, d), lambda i, j: (0, 0)), vec_spec, vec_spec],
        prologue, w, [], [], lambda acc: acc, m, d, n, tm, tn, out_dtype)


def _residual_epilogue_specs(x2, gate, seq, tm, tn):
    bsz, d = gate.shape
    tiles_per_seq = seq // tm
    ins = [x2, gate.reshape(bsz, 1, d)]
    specs = [pl.BlockSpec((tm, tn), lambda i, j: (i, j)),
             pl.BlockSpec((1, 1, tn), lambda i, j: (i // tiles_per_seq, 0, j))]
    return ins, specs


def _residual_epilogue(acc, x_ref, gate_ref):
    return x_ref[...] + gate_ref[0] * acc


def _out_proj(name, a, w, x2, gate, seq, tm=512, tn=1024):
    m, k = a.shape
    n = w.shape[1]
    tm = _row_tile(seq, tm)
    tn = _col_tile(n, tn)
    e_ins, e_specs = _residual_epilogue_specs(x2, gate, seq, tm, tn)
    return _fused_matmul(name, [a], [pl.BlockSpec((tm, k), lambda i, j: (i, 0))], None, w,
                         e_ins, e_specs, _residual_epilogue, m, k, n, tm, tn, F32)


def _na_bias_table(rpb, rows):
    rb, w = NA_ROW_BLOCK, GRID_W
    half = NA_ROWS // 2
    qc = jnp.arange(w)
    col_start = jnp.clip(qc - NA_COLS // 2, 0, w - NA_COLS)
    kc = jnp.arange(w)
    col_ok = (kc[None, :] >= col_start[:, None]) & (kc[None, :] < col_start[:, None] + NA_COLS)
    dc = jnp.clip(kc[None, :] - qc[:, None] + NA_COLS - 1, 0, 2 * NA_COLS - 2)
    tables = []
    for r0 in (0, rb, rows - rb):
        r = r0 + jnp.arange(rb)
        kr = r0 - half + jnp.arange(2 * rb)
        row_start = jnp.clip(r - half, 0, rows - NA_ROWS)
        row_ok = (kr[None, :] >= row_start[:, None]) & (kr[None, :] < row_start[:, None] + NA_ROWS)
        dr = jnp.clip(kr[None, :] - r[:, None] + NA_ROWS - 1, 0, 2 * NA_ROWS - 2)
        b = rpb.astype(F32)[:, dr[:, None, :, None], dc[None, :, None, :]]
        ok = row_ok[:, None, :, None] & col_ok[None, :, None, :]
        b = jnp.where(ok[None], b, NEG_INF)
        tables.append(b.reshape(rpb.shape[0], rb * w, 2 * rb * w))
    return jnp.stack(tables, axis=1)


def _na_attention(qkv, g, bias, bsz, seq):
    e = g.shape[1]
    heads = e // NA_HEAD_DIM
    rows = seq // GRID_W
    tq = NA_ROW_BLOCK * GRID_W
    nblk = rows // NA_ROW_BLOCK
    assert nblk >= 2 and rows % NA_ROW_BLOCK == 0
    half = tq // 2
    scale = NA_HEAD_DIM ** -0.5

    def body(q_ref, kp_ref, kc_ref, kn_ref, vp_ref, vc_ref, vn_ref, g_ref, b_ref, o_ref):
        q = q_ref[...]
        kcat = jnp.concatenate([kp_ref[half:, :], kc_ref[...], kn_ref[:half, :]], axis=0)
        vcat = jnp.concatenate([vp_ref[half:, :], vc_ref[...], vn_ref[:half, :]], axis=0)
        s = _dot_nt(q, kcat) * scale + b_ref[0, 0]
        p = jnp.exp(s - jnp.max(s, axis=-1, keepdims=True))
        denom = jnp.sum(p, axis=-1, keepdims=True)
        o = _dot(p.astype(BF16), vcat) / denom
        o_ref[...] = (o * _silu(g_ref[...])).astype(BF16)

    def tok(off):
        def index(b, h, i):
            return (b * nblk + jnp.clip(i + off, 0, nblk - 1), 0)
        return index

    def spec(col0, off):
        index = tok(off)
        return pl.BlockSpec((tq, NA_HEAD_DIM), lambda b, h, i: (index(b, h, i)[0], col0 + h))

    def variant(i):
        return jnp.where(i == 0, 0, jnp.where(i == nblk - 1, 2, 1))

    return pl.pallas_call(
        body,
        grid=(bsz, heads, nblk),
        in_specs=[spec(0, 0),
                  spec(heads, -1), spec(heads, 0), spec(heads, 1),
                  spec(2 * heads, -1), spec(2 * heads, 0), spec(2 * heads, 1),
                  spec(0, 0),
                  pl.BlockSpec((1, 1, tq, 2 * tq), lambda b, h, i: (h, variant(i), 0, 0))],
        out_specs=spec(0, 0),
        out_shape=jax.ShapeDtypeStruct((bsz * seq, e), BF16),
        compiler_params=_params("parallel", "parallel", "arbitrary"),
        name="na_attention",
    )(qkv, qkv, qkv, qkv, qkv, qkv, qkv, g, bias)


def _na_layer(x2, bsz, seq, norm_w, scale, shift, gate, w_in, rpb, w_out):
    e = w_out.shape[0]
    w_in = w_in.astype(BF16)
    qkv = _in_proj("na_in_qkv", x2, seq, norm_w, scale, shift, w_in[:, :3 * e], BF16)
    g = _in_proj("na_in_gate", x2, seq, norm_w, scale, shift, w_in[:, 3 * e:], F32)
    bias = _na_bias_table(rpb, seq // GRID_W)
    a = _na_attention(qkv, g, bias, bsz, seq)
    return _out_proj("na_out", a, w_out.astype(BF16), x2, gate, seq)


S5_GROUPS_PER_LANE_BLOCK = LANES // S5_GROUP
S5_STATES_PER_LANE_BLOCK = S5_GROUPS_PER_LANE_BLOCK * S5_STATE
S5_LANE_BLOCKS_PER_STEP = 8


def _s5_discretize(a_re, a_im, log_step, b_re_t, b_im_t):
    n_in, groups, states = b_re_t.shape

    def body(are_ref, aim_ref, ls_ref, bre_ref, bim_ref, lre_ref, lim_ref, bbre_ref, bbim_ref):
        dt = jnp.exp(ls_ref[...])
        are, aim = are_ref[...], aim_ref[...]
        mag = jnp.exp(are * dt)
        lam_re = mag * jnp.cos(aim * dt)
        lam_im = mag * jnp.sin(aim * dt)
        den = are * are + aim * aim
        nr = lam_re - 1.0
        r_re = (nr * are + lam_im * aim) / den
        r_im = (lam_im * are - nr * aim) / den
        lre_ref[...] = lam_re
        lim_ref[...] = lam_im
        for i in range(n_in):
            bbre_ref[i] = r_re * bre_ref[i] - r_im * bim_ref[i]
            bbim_ref[i] = r_re * bim_ref[i] + r_im * bre_ref[i]

    gp = jax.ShapeDtypeStruct((groups, states), F32)
    igp = jax.ShapeDtypeStruct((n_in, groups, states), F32)
    return pl.pallas_call(body, out_shape=(gp, gp, igp, igp), name="s5_discretize",
                          compiler_params=pltpu.CompilerParams(vmem_limit_bytes=VMEM_LIMIT),
                          )(a_re, a_im, log_step, b_re_t, b_im_t)


def _s5_block_diag_in(bb_re_t, bb_im_t):
    n_in, groups, states = bb_re_t.shape
    gl = S5_GROUPS_PER_LANE_BLOCK
    eye = jnp.eye(gl, dtype=F32)

    def one(t):
        t = t.transpose(1, 0, 2).reshape(groups // gl, gl, n_in, states)
        return jnp.einsum('jgip,gh->jgihp', t, eye).reshape(groups // gl, gl * n_in, gl * states)

    return jnp.concatenate([one(bb_re_t), one(bb_im_t)], axis=-1).astype(BF16)


def _s5_block_diag_out(c_re, c_im):
    groups, n_in, states = c_re.shape
    gl = S5_GROUPS_PER_LANE_BLOCK
    eye = jnp.eye(gl, dtype=F32)

    def one(t):
        t = t.astype(F32).reshape(groups // gl, gl, n_in, states)
        return jnp.einsum('jgip,gh->jgphi', t, eye).reshape(groups // gl, gl * states, gl * n_in)

    return jnp.concatenate([one(c_re), -one(c_im)], axis=1).astype(BF16)


def _s5_scan(ug, bsz, seq, e, wb_f, wb_b, wc, lam_f, lam_b, d_skip, tb=128):
    nj = S5_LANE_BLOCKS_PER_STEP
    ns = S5_STATES_PER_LANE_BLOCK
    cw = nj * LANES
    nre = ns // LANES
    nlb = 2 * nre
    tb = _row_tile(seq, tb)
    nt = seq // tb
    n_cb = e // cw

    def body(uf_ref, ub_ref, wbf_ref, wbb_ref, wc_ref, lamf_ref, lamb_ref, d_ref,
             yf_ref, yb_ref, xf_ref, xb_ref, sf_ref, sb_ref):
        @pl.when(pl.program_id(2) == 0)
        def _():
            sf_ref[...] = jnp.zeros_like(sf_ref)
            sb_ref[...] = jnp.zeros_like(sb_ref)

        for j in range(nj):
            cols = slice(j * LANES, (j + 1) * LANES)
            bu_f = _dot(uf_ref[:, cols].astype(BF16), wbf_ref[j])
            bu_b = _dot(ub_ref[:, cols].astype(BF16), wbb_ref[j])
            for l in range(nlb):
                xf_ref[l, j * tb:(j + 1) * tb, :] = bu_f[:, l * LANES:(l + 1) * LANES]
                xb_ref[l, j * tb:(j + 1) * tb, :] = bu_b[:, l * LANES:(l + 1) * LANES]

        def lam_blocks(lam_ref):
            return [lam_ref[0, l // nre, :, (l % nre) * LANES:(l % nre + 1) * LANES] for l in range(nlb)]

        lam_f, lam_b = lam_blocks(lamf_ref), lam_blocks(lamb_ref)

        def advance(x_ref, rows, lam, state):
            new = []
            for l in range(nre):
                lr, li = lam[l], lam[nre + l]
                sr, si = state[l], state[nre + l]
                new.append((lr * sr - li * si + x_ref[l, rows, :], lr * si + li * sr + x_ref[nre + l, rows, :]))
            state = [p[0] for p in new] + [p[1] for p in new]
            for l in range(nlb):
                x_ref[l, rows, :] = state[l]
            return state

        def step(t, carry):
            sf, sb = carry
            sf = advance(xf_ref, pl.ds(t, nj, stride=tb), lam_f, sf)
            sb = advance(xb_ref, pl.ds(tb - 1 - t, nj, stride=tb), lam_b, sb)
            return sf, sb

        init = ([sf_ref[l] for l in range(nlb)], [sb_ref[l] for l in range(nlb)])
        sf, sb = lax.fori_loop(0, tb, step, init, unroll=8)
        for l in range(nlb):
            sf_ref[l] = sf[l]
            sb_ref[l] = sb[l]

        for j in range(nj):
            cols = slice(j * LANES, (j + 1) * LANES)
            rows = slice(j * tb, (j + 1) * tb)
            x_f = jnp.concatenate([xf_ref[l, rows, :] for l in range(nlb)], axis=-1).astype(BF16)
            x_b = jnp.concatenate([xb_ref[l, rows, :] for l in range(nlb)], axis=-1).astype(BF16)
            yf_ref[:, cols] = _dot(x_f, wc_ref[j]) + d_ref[:, cols] * uf_ref[:, cols]
            yb_ref[:, cols] = _dot(x_b, wc_ref[j])

    def fwd(b, c, t):
        return (b * nt + t, c)

    def bwd(b, c, t):
        return (b * nt + nt - 1 - t, c)

    w_in_spec = pl.BlockSpec((nj, LANES, 2 * ns), lambda b, c, t: (c, 0, 0))
    lam_spec = pl.BlockSpec((1, 2, nj, ns), lambda b, c, t: (c, 0, 0, 0))
    out = jax.ShapeDtypeStruct((bsz * seq, e), F32)
    return pl.pallas_call(
        body,
        grid=(bsz, n_cb, nt),
        in_specs=[pl.BlockSpec((tb, cw), fwd), pl.BlockSpec((tb, cw), bwd),
                  w_in_spec, w_in_spec,
                  pl.BlockSpec((nj, 2 * ns, LANES), lambda b, c, t: (c, 0, 0)),
                  lam_spec, lam_spec,
                  pl.BlockSpec((1, cw), lambda b, c, t: (0, c))],
        out_specs=(pl.BlockSpec((tb, cw), fwd), pl.BlockSpec((tb, cw), bwd)),
        out_shape=(out, out),
        scratch_shapes=[pltpu.VMEM((nlb, nj * tb, LANES), F32), pltpu.VMEM((nlb, nj * tb, LANES), F32),
                        pltpu.VMEM((nlb, nj, LANES), F32), pltpu.VMEM((nlb, nj, LANES), F32)],
        compiler_params=_params("parallel", "parallel", "arbitrary"),
        name="s5_scan",
    )(ug, ug, wb_f, wb_b, wc, lam_f, lam_b, d_skip.reshape(1, e))


def _s5_glu(yf, yb, ug, w_glu, b_glu, seq, tm=256, tn=1024):
    m, e = yf.shape
    tm = _row_tile(seq, tm)
    tn = _col_tile(e, tn)
    g_col0 = e // tn

    def prologue(h_ref, yf_ref, yb_ref):
        h_ref[...] = jax.nn.gelu(yf_ref[...] + yb_ref[...]).astype(BF16)

    def epilogue(acc, yf_ref, yb_ref, g_ref, b_ref):
        y = jax.nn.gelu(yf_ref[...] + yb_ref[...])
        return y * jax.nn.sigmoid(acc + b_ref[...]) * _silu(g_ref[...])

    row_spec = pl.BlockSpec((tm, e), lambda i, j: (i, 0))
    tile_spec = pl.BlockSpec((tm, tn), lambda i, j: (i, j))
    return _fused_matmul(
        "s5_glu", [yf, yb], [row_spec, row_spec], prologue, w_glu,
        [yf, yb, ug, b_glu.reshape(1, e)],
        [tile_spec, tile_spec, pl.BlockSpec((tm, tn), lambda i, j: (i, g_col0 + j)),
         pl.BlockSpec((1, tn), lambda i, j: (0, j))],
        epilogue, m, e, e, tm, tn, BF16)


def _s5_layer(x2, bsz, seq, norm_w, scale, shift, gate, w_in, b_re, b_im, c_re, c_im, d_skip,
              a_re_f, a_im_f, a_re_b, a_im_b, log_step_f, log_step_b, w_glu, b_glu, w_out):
    e = w_out.shape[0]
    groups = e // S5_GROUP
    ug = _in_proj("s5_in", x2, seq, norm_w, scale, shift, w_in.astype(BF16), F32)
    b_re_t = b_re.astype(F32).transpose(2, 0, 1)
    b_im_t = b_im.astype(F32).transpose(2, 0, 1)
    f32 = lambda t: t.astype(F32)
    lre_f, lim_f, bbre_f, bbim_f = _s5_discretize(f32(a_re_f), f32(a_im_f), f32(log_step_f), b_re_t, b_im_t)
    lre_b, lim_b, bbre_b, bbim_b = _s5_discretize(f32(a_re_b), f32(a_im_b), f32(log_step_b), b_re_t, b_im_t)
    nj, ns = S5_LANE_BLOCKS_PER_STEP, S5_STATES_PER_LANE_BLOCK
    n_cb = groups // (nj * S5_GROUPS_PER_LANE_BLOCK)
    lam = lambda re, im: jnp.stack([re.reshape(n_cb, nj, ns), im.reshape(n_cb, nj, ns)], axis=1)
    yf, yb = _s5_scan(ug, bsz, seq, e,
                      _s5_block_diag_in(bbre_f, bbim_f), _s5_block_diag_in(bbre_b, bbim_b),
                      _s5_block_diag_out(c_re, c_im), lam(lre_f, lim_f), lam(lre_b, lim_b),
                      d_skip.astype(F32).reshape(e))
    a = _s5_glu(yf, yb, ug, w_glu.astype(BF16), b_glu.astype(F32), seq)
    return _out_proj("s5_out", a, w_out.astype(BF16), x2, gate, seq)


def _split3(x):
    p1 = x.astype(BF16)
    r1 = x - p1.astype(F32)
    p2 = r1.astype(BF16)
    p3 = (r1 - p2.astype(F32)).astype(BF16)
    return p1, p2, p3


def _gla_core(proj, lowrank, w2_f, bg_f, w2_b, bg_b, bsz, seq, e, chunks_per_step=4):
    heads = e // GLA_DV
    dk, dv, ch = GLA_DK, GLA_DV, GLA_CHUNK
    rt = min(chunks_per_step * ch, seq)
    ncs = rt // ch
    nb = seq // rt
    q_scale = dk ** -0.5

    def direction(forward, q_ref, k_ref, v_ref, lr_ref, w2_ref, bg_ref, o_ref, st_ref):
        row = lax.broadcasted_iota(jnp.int32, (rt, rt), 0)
        col = lax.broadcasted_iota(jnp.int32, (rt, rt), 1)
        shift = ch.bit_length() - 1
        same = (row >> shift) == (col >> shift)
        if forward:
            cum_mask = same & (col <= row)
            score_mask = cum_mask
        else:
            cum_mask = same & (col >= row)
            score_mask = same & (col > row)
        gk = _log_sigmoid(_dot(lr_ref[...].astype(BF16), w2_ref[...]) + bg_ref[...]) / GLA_GATE_NORM
        tri = jnp.where(cum_mask, 1.0, 0.0).astype(BF16)
        g1, g2, g3 = _split3(gk)
        bcum = _dot(tri, g1) + _dot(tri, g2) + _dot(tri, g3)
        q = q_ref[...] * q_scale
        k = k_ref[...]
        v = v_ref[...].astype(BF16)
        q_s = (q * jnp.exp(bcum)).astype(BF16)
        k_s = (k * jnp.exp(-bcum)).astype(BF16)
        scores = jnp.where(score_mask, _dot_nt(q_s, k_s), 0.0)
        o_intra = _dot(scores.astype(BF16), v)
        order = range(ncs) if forward else range(ncs - 1, -1, -1)
        for c in order:
            rows = slice(c * ch, (c + 1) * ch)
            last = c * ch + ch - 1 if forward else c * ch
            b_last = bcum[last:last + 1, :]
            state = st_ref[...]
            o_inter = _dot_nt(q_s[rows], state.astype(BF16))
            o_ref[rows, :] = o_intra[rows] + o_inter
            k_end = (k[rows] * jnp.exp(b_last - bcum[rows])).astype(BF16)
            st_ref[...] = state * jnp.exp(b_last) + _dot_tn(v[rows], k_end)

    def body(qf, kf, vf, lf, qb, kb, vb, lb, w2f, bgf, w2b, bgb, of, ob, stf, stb):
        @pl.when(pl.program_id(2) == 0)
        def _():
            stf[...] = jnp.zeros_like(stf)
            stb[...] = jnp.zeros_like(stb)

        direction(True, qf, kf, vf, lf, w2f, bgf, of, stf)
        direction(False, qb, kb, vb, lb, w2b, bgb, ob, stb)

    def tok(forward):
        return (lambda b, h, n: b * nb + n) if forward else (lambda b, h, n: b * nb + nb - 1 - n)

    def specs(forward):
        t = tok(forward)
        return [pl.BlockSpec((rt, dk), lambda b, h, n: (t(b, h, n), h)),
                pl.BlockSpec((rt, dk), lambda b, h, n: (t(b, h, n), heads + h)),
                pl.BlockSpec((rt, dv), lambda b, h, n: (t(b, h, n), heads + h)),
                pl.BlockSpec((rt, LANES), lambda b, h, n: (t(b, h, n), 0))]

    w2_spec = pl.BlockSpec((LANES, dk), lambda b, h, n: (0, h))
    bg_spec = pl.BlockSpec((1, dk), lambda b, h, n: (0, h))
    out = jax.ShapeDtypeStruct((bsz * seq, e), F32)
    tf, tbk = tok(True), tok(False)
    return pl.pallas_call(
        body,
        grid=(bsz, heads, nb),
        in_specs=specs(True) + specs(False) + [w2_spec, bg_spec, w2_spec, bg_spec],
        out_specs=(pl.BlockSpec((rt, dv), lambda b, h, n: (tf(b, h, n), h)),
                   pl.BlockSpec((rt, dv), lambda b, h, n: (tbk(b, h, n), h))),
        out_shape=(out, out),
        scratch_shapes=[pltpu.VMEM((dv, dk), F32), pltpu.VMEM((dv, dk), F32)],
        compiler_params=_params("parallel", "parallel", "arbitrary"),
        name="gla_core",
    )(proj, proj, proj, lowrank, proj, proj, proj, lowrank, w2_f, bg_f.reshape(1, -1),
      w2_b, bg_b.reshape(1, -1))


def _gla_out(of, ob, proj, norm_w, w_out, x2, gate, seq, tm=256, tn=1024):
    m, e = of.shape
    n = w_out.shape[1]
    heads = e // GLA_DV
    tm = _row_tile(seq, tm)
    tn = _col_tile(n, tn)
    g_block = (proj.shape[1] - e) // e

    def prologue(h_ref, of_ref, ob_ref, g_ref, nw_ref):
        for h in range(heads):
            cols = slice(h * GLA_DV, (h + 1) * GLA_DV)
            o = of_ref[:, cols] + ob_ref[:, cols]
            o = o * lax.rsqrt(jnp.mean(o * o, axis=-1, keepdims=True) + EPS) * nw_ref[...]
            h_ref[:, cols] = (o * _silu(g_ref[:, cols])).astype(BF16)

    row_spec = pl.BlockSpec((tm, e), lambda i, j: (i, 0))
    e_ins, e_specs = _residual_epilogue_specs(x2, gate, seq, tm, tn)
    return _fused_matmul(
        "gla_out", [of, ob, proj, norm_w.reshape(1, GLA_DV)],
        [row_spec, row_spec, pl.BlockSpec((tm, e), lambda i, j: (i, g_block)),
         pl.BlockSpec((1, GLA_DV), lambda i, j: (0, 0))],
        prologue, w_out, e_ins, e_specs, _residual_epilogue, m, e, n, tm, tn, F32)


def _gla_layer(x2, bsz, seq, norm_w, scale, shift, gate, w_in, w1_f, w2_f, bg_f, w1_b, w2_b, bg_b,
               gla_norm_w, w_out):
    e = w_out.shape[0]
    d = w_in.shape[0]
    rank = w1_f.shape[1]
    proj = _in_proj("gla_in", x2, seq, norm_w, scale, shift, w_in.astype(BF16), F32)
    w1 = jnp.zeros((d, LANES), BF16).at[:, :rank].set(w1_f.astype(BF16))
    w1 = w1.at[:, rank:2 * rank].set(w1_b.astype(BF16))
    lowrank = _in_proj("gla_in_gate_rank", x2, seq, norm_w, scale, shift, w1, F32, tn=LANES)
    kw = w2_f.shape[1]
    w2f = jnp.zeros((LANES, kw), BF16).at[:rank].set(w2_f.astype(BF16))
    w2b = jnp.zeros((LANES, kw), BF16).at[rank:2 * rank].set(w2_b.astype(BF16))
    of, ob = _gla_core(proj, lowrank, w2f, bg_f.astype(F32), w2b, bg_b.astype(F32), bsz, seq, e)
    return _gla_out(of, ob, proj, gla_norm_w.astype(F32), w_out.astype(BF16), x2, gate, seq)


def _final_norm(x2, w, tm=512):
    m, d = x2.shape
    tm = _row_tile(m, tm)

    def body(x_ref, w_ref, o_ref):
        xv = x_ref[...]
        o_ref[...] = xv * lax.rsqrt(jnp.mean(xv * xv, axis=-1, keepdims=True) + EPS) * w_ref[...]

    return pl.pallas_call(
        body, grid=(m // tm,),
        in_specs=[pl.BlockSpec((tm, d), lambda i: (i, 0)), pl.BlockSpec((1, d), lambda i: (0, 0))],
        out_specs=pl.BlockSpec((tm, d), lambda i: (i, 0)),
        out_shape=jax.ShapeDtypeStruct((m, d), F32),
        compiler_params=_params("parallel"),
        name="final_norm",
    )(x2, w.reshape(1, d).astype(F32))


def kernel(x, c, mod_w, mod_b, norm_w, na_w_in, na_rpb, na_w_out, s5_w_in, s5_b_re, s5_b_im, s5_c_re, s5_c_im, s5_d, s5_a_re_fwd, s5_a_im_fwd, s5_a_re_bwd, s5_a_im_bwd, s5_log_step_fwd, s5_log_step_bwd, s5_w_glu, s5_b_glu, s5_w_out, gla_w_in, gla_gk_w1_fwd, gla_gk_w2_fwd, gla_gk_b_fwd, gla_gk_w1_bwd, gla_gk_w2_bwd, gla_gk_b_bwd, gla_norm_w, gla_w_out, final_norm_w):
    bsz, seq, d = x.shape
    depth = mod_w.shape[0]
    mod = _modulation(c.astype(F32), mod_w, mod_b.astype(F32))
    x2 = x.astype(F32).reshape(bsz * seq, d)
    for i in range(depth):
        kind, j = i % 3, i // 3
        shift, scale, gate = mod[i, :, :d], mod[i, :, d:2 * d], mod[i, :, 2 * d:]
        nw = norm_w[i].astype(F32)
        if kind == 0:
            x2 = _na_layer(x2, bsz, seq, nw, scale, shift, gate, na_w_in[j], na_rpb[j], na_w_out[j])
        elif kind == 1:
            x2 = _s5_layer(x2, bsz, seq, nw, scale, shift, gate, s5_w_in[j], s5_b_re[j], s5_b_im[j],
                           s5_c_re[j], s5_c_im[j], s5_d[j], s5_a_re_fwd[j], s5_a_im_fwd[j],
                           s5_a_re_bwd[j], s5_a_im_bwd[j], s5_log_step_fwd[j], s5_log_step_bwd[j],
                           s5_w_glu[j], s5_b_glu[j], s5_w_out[j])
        else:
            x2 = _gla_layer(x2, bsz, seq, nw, scale, shift, gate, gla_w_in[j], gla_gk_w1_fwd[j],
                            gla_gk_w2_fwd[j], gla_gk_b_fwd[j], gla_gk_w1_bwd[j], gla_gk_w2_bwd[j],
                            gla_gk_b_bwd[j], gla_norm_w[j], gla_w_out[j])
    return _final_norm(x2, final_norm_w).reshape(bsz, seq, d)
```

```python
import functools
import math

import jax
import jax.numpy as jnp
from jax import lax
from jax.experimental import pallas as pl
from jax.experimental.pallas import tpu as pltpu

EPS = 1e-6
GRID_W = 64
NA_HEAD_DIM = 128
NA_ROWS = 8
NA_COLS = 16
NA_ROW_BLOCK = 8
NA_HEADS_PER_STEP = 2
NA_PIPELINE_GROUP = 4
S5_GROUP = 16
S5_STATE = 64
GLA_DV = 512
GLA_DK = 256
GLA_GATE_RANK = 16
GLA_GATE_NORM = 16.0
GLA_CHUNK = 64
NEG_INF = -1e30

V7X_VMEM_BYTES = 64 * 1024 * 1024
VMEM_LIMIT = V7X_VMEM_BYTES - 8 * 1024 * 1024
LANES = 128
SUBLANES = 8
BF16 = jnp.bfloat16
F32 = jnp.float32


def _params(*sem):
    return pltpu.CompilerParams(dimension_semantics=sem, vmem_limit_bytes=VMEM_LIMIT)


def _silu(x):
    return x * jax.nn.sigmoid(x)


def _log_sigmoid(x):
    return jnp.minimum(x, 0.0) - jnp.log1p(jnp.exp(-jnp.abs(x)))


def _dot(a, b):
    return jnp.dot(a, b, preferred_element_type=F32)


def _dot_nt(a, b):
    return lax.dot_general(a, b, (((1,), (1,)), ((), ())), preferred_element_type=F32)


def _dot_tn(a, b):
    return lax.dot_general(a, b, (((0,), (0,)), ((), ())), preferred_element_type=F32)


def _fused_matmul(name, a_ins, a_specs, prologue, w, e_ins, e_specs, epilogue,
                  m, k, n, tm, tn, out_dtype):
    n_a, n_e = len(a_ins), len(e_ins)

    def body(*refs):
        a_refs = refs[:n_a]
        w_ref = refs[n_a]
        e_refs = refs[n_a + 1:n_a + 1 + n_e]
        o_ref = refs[n_a + 1 + n_e]
        if prologue is None:
            lhs = a_refs[0][...]
        else:
            h_ref = refs[n_a + 2 + n_e]

            @pl.when(pl.program_id(1) == 0)
            def _():
                prologue(h_ref, *a_refs)

            lhs = h_ref[...]
        acc = _dot(lhs, w_ref[...])
        o_ref[...] = epilogue(acc, *e_refs).astype(out_dtype)

    scratch = [] if prologue is None else [pltpu.VMEM((tm, k), BF16)]
    return pl.pallas_call(
        body,
        grid=(m // tm, n // tn),
        in_specs=list(a_specs) + [pl.BlockSpec((k, tn), lambda i, j: (0, j))] + list(e_specs),
        out_specs=pl.BlockSpec((tm, tn), lambda i, j: (i, j)),
        out_shape=jax.ShapeDtypeStruct((m, n), out_dtype),
        scratch_shapes=scratch,
        compiler_params=_params("parallel", "arbitrary"),
        name=name,
    )(*a_ins, w, *e_ins)


def _row_tile(seq, want):
    tm = min(want, seq)
    assert seq % tm == 0
    return tm


def _col_tile(n, want):
    tn = min(want, n)
    assert n % tn == 0
    return tn


def _modulation(c, mod_w, mod_b):
    depth, d, n = mod_w.shape
    bsz = c.shape[0]
    rows = 8
    c_pad = jnp.zeros((rows, d), F32).at[:bsz].set(c)
    tn = _col_tile(n, 512)

    def body(c_ref, w_ref, b_ref, o_ref):
        act = _silu(c_ref[...]).astype(BF16)
        o_ref[0] = _dot(act, w_ref[0].astype(BF16)) + b_ref[0]

    out = pl.pallas_call(
        body,
        grid=(depth, n // tn),
        in_specs=[pl.BlockSpec((rows, d), lambda i, j: (0, 0)),
                  pl.BlockSpec((1, d, tn), lambda i, j: (i, 0, j)),
                  pl.BlockSpec((1, 1, tn), lambda i, j: (i, 0, j))],
        out_specs=pl.BlockSpec((1, rows, tn), lambda i, j: (i, 0, j)),
        out_shape=jax.ShapeDtypeStruct((depth, rows, n), F32),
        compiler_params=_params("parallel", "parallel"),
        name="modulation",
    )(c_pad, mod_w, mod_b.reshape(depth, 1, n))
    return out[:, :bsz]


def _in_proj(name, x2, seq, norm_w, scale, shift, w, out_dtype, tm=512, tn=1024):
    m, d = x2.shape
    n = w.shape[1]
    tm = _row_tile(seq, tm)
    tn = _col_tile(n, tn)
    tiles_per_seq = seq // tm
    bsz = m // seq

    def prologue(h_ref, x_ref, nw_ref, sc_ref, sh_ref):
        xv = x_ref[...]
        y = xv * lax.rsqrt(jnp.mean(xv * xv, axis=-1, keepdims=True) + EPS) * nw_ref[...]
        h_ref[...] = (y * (1.0 + sc_ref[0]) + sh_ref[0]).astype(BF16)

    vec_spec = pl.BlockSpec((1, 1, d), lambda i, j: (i // tiles_per_seq, 0, 0))
    return _fused_matmul(
        name,
        [x2, norm_w.reshape(1, d), scale.reshape(bsz, 1, d), shift.reshape(bsz, 1, d)],
        [pl.BlockSpec((tm, d), lambda i, j: (i, 0)),
         pl.BlockSpec((1, d), lambda i, j: (0, 0)), vec_spec, vec_spec],
        prologue, w, [], [], lambda acc: acc, m, d, n, tm, tn, out_dtype)


def _residual_epilogue_specs(x2, gate, seq, tm, tn):
    bsz, d = gate.shape
    tiles_per_seq = seq // tm
    ins = [x2, gate.reshape(bsz, 1, d)]
    specs = [pl.BlockSpec((tm, tn), lambda i, j: (i, j)),
             pl.BlockSpec((1, 1, tn), lambda i, j: (i // tiles_per_seq, 0, j))]
    return ins, specs


def _residual_epilogue(acc, x_ref, gate_ref):
    return x_ref[...] + gate_ref[0] * acc


def _out_proj(name, a, w, x2, gate, seq, tm=512, tn=1024):
    m, k = a.shape
    n = w.shape[1]
    tm = _row_tile(seq, tm)
    tn = _col_tile(n, tn)
    e_ins, e_specs = _residual_epilogue_specs(x2, gate, seq, tm, tn)
    return _fused_matmul(name, [a], [pl.BlockSpec((tm, k), lambda i, j: (i, 0))], None, w,
                         e_ins, e_specs, _residual_epilogue, m, k, n, tm, tn, F32)


def _na_bias_table(rpb):
    w = GRID_W
    qc = jnp.arange(w)
    kc = jnp.arange(w)
    col_start = jnp.clip(qc - NA_COLS // 2, 0, w - NA_COLS)
    col_ok = (kc[None, :] >= col_start[:, None]) & (kc[None, :] < col_start[:, None] + NA_COLS)
    dc = kc[None, :] - qc[:, None] + NA_COLS - 1
    onehot = ((dc[None] == jnp.arange(2 * NA_COLS - 1)[:, None, None]) & col_ok[None]).astype(F32)
    cexp = jnp.einsum('hdc,cqk->hdqk', rpb.astype(F32), onehot, precision=lax.Precision.HIGHEST)
    cexp = jnp.where(col_ok[None, None], cexp, NEG_INF)
    tables = [jnp.concatenate([cexp[:, t + m] for m in range(NA_ROWS)], axis=-1) for t in range(NA_ROWS)]
    return jnp.stack(tables, axis=1)


def _na_attention(qkv, g, bias, bsz, seq):
    e = g.shape[1]
    heads = e // NA_HEAD_DIM
    rows = seq // GRID_W
    tq = NA_ROW_BLOCK * GRID_W
    nblk = rows // NA_ROW_BLOCK
    assert nblk >= 2 and rows % NA_ROW_BLOCK == 0
    half = tq // 2
    hp = NA_HEADS_PER_STEP
    hw = hp * NA_HEAD_DIM
    win = NA_ROWS * GRID_W
    half_rows = NA_ROWS // 2
    scale = NA_HEAD_DIM ** -0.5
    assert heads % hp == 0 and NA_ROW_BLOCK == NA_ROWS

    def body(q_ref, kp_ref, kc_ref, kn_ref, vp_ref, vc_ref, vn_ref, g_ref, b_ref, o_ref,
             kcat_ref, vcat_ref):
        kcat_ref[:half] = kp_ref[...]
        kcat_ref[half:half + tq] = kc_ref[...]
        kcat_ref[half + tq:] = kn_ref[...]
        vcat_ref[:half] = vp_ref[...]
        vcat_ref[half:half + tq] = vc_ref[...]
        vcat_ref[half + tq:] = vn_ref[...]
        def attend(r0):
            def place(qr, h):
                r = r0 + qr
                row_start = min(max(r - half_rows, 0), rows - NA_ROWS)
                start = (row_start - r0 + half_rows) * GRID_W
                return (slice(qr * GRID_W, (qr + 1) * GRID_W), slice(h * NA_HEAD_DIM, (h + 1) * NA_HEAD_DIM),
                        slice(start, start + win), row_start - r + NA_ROWS - 1)

            def scores(qr, h):
                q_rows, lanes, keys, table = place(qr, h)
                return _dot_nt(q_ref[q_rows, lanes], kcat_ref[keys, lanes]) * scale + b_ref[h, table]

            def finish(qr, h, s):
                q_rows, lanes, keys, _ = place(qr, h)
                p = jnp.exp(s - jnp.max(s, axis=-1, keepdims=True))
                denom = jnp.sum(p, axis=-1, keepdims=True)
                o = _dot(p.astype(BF16), vcat_ref[keys, lanes]) / denom
                o_ref[q_rows, lanes] = (o * _silu(g_ref[q_rows, lanes])).astype(BF16)

            work = [(qr, h) for qr in range(NA_ROW_BLOCK) for h in range(hp)]
            groups = [work[i:i + NA_PIPELINE_GROUP] for i in range(0, len(work), NA_PIPELINE_GROUP)]
            pending = [scores(*w) for w in groups[0]]
            for gi, grp in enumerate(groups):
                ahead = [scores(*w) for w in groups[gi + 1]] if gi + 1 < len(groups) else []
                for w, s in zip(grp, pending):
                    finish(*w, s)
                pending = ahead

        blk = pl.program_id(2)
        pl.when(blk == 0)(lambda: attend(0))
        pl.when(blk == nblk - 1)(lambda: attend(rows - NA_ROW_BLOCK))
        pl.when((blk > 0) & (blk < nblk - 1))(lambda: attend(NA_ROW_BLOCK))

    def cur(col0):
        return pl.BlockSpec((tq, hw), lambda b, h, i: (b * nblk + i, col0 + h))

    def above(col0):
        return pl.BlockSpec((half, hw), lambda b, h, i: (jnp.maximum(2 * (b * nblk + i) - 1, 0), col0 + h))

    def below(col0):
        last = 2 * bsz * nblk - 1
        return pl.BlockSpec((half, hw), lambda b, h, i: (jnp.minimum(2 * (b * nblk + i) + 2, last), col0 + h))

    kcol, vcol = heads // hp, 2 * heads // hp
    return pl.pallas_call(
        body,
        grid=(bsz, heads // hp, nblk),
        in_specs=[cur(0),
                  above(kcol), cur(kcol), below(kcol),
                  above(vcol), cur(vcol), below(vcol),
                  cur(0),
                  pl.BlockSpec((hp, NA_ROWS, GRID_W, win), lambda b, h, i: (h, 0, 0, 0))],
        out_specs=cur(0),
        out_shape=jax.ShapeDtypeStruct((bsz * seq, e), BF16),
        scratch_shapes=[pltpu.VMEM((2 * tq, hw), BF16), pltpu.VMEM((2 * tq, hw), BF16)],
        compiler_params=_params("parallel", "parallel", "arbitrary"),
        name="na_attention",
    )(qkv, qkv, qkv, qkv, qkv, qkv, qkv, g, bias)


def _na_layer(x2, bsz, seq, norm_w, scale, shift, gate, w_in, rpb, w_out):
    e = w_out.shape[0]
    w_in = w_in.astype(BF16)
    qkv = _in_proj("na_in_qkv", x2, seq, norm_w, scale, shift, w_in[:, :3 * e], BF16)
    g = _in_proj("na_in_gate", x2, seq, norm_w, scale, shift, w_in[:, 3 * e:], F32)
    bias = _na_bias_table(rpb)
    a = _na_attention(qkv, g, bias, bsz, seq)
    return _out_proj("na_out", a, w_out.astype(BF16), x2, gate, seq)


S5_GROUPS_PER_LANE_BLOCK = LANES // S5_GROUP
S5_STATES_PER_LANE_BLOCK = S5_GROUPS_PER_LANE_BLOCK * S5_STATE
S5_LANE_BLOCKS_PER_STEP = 8


def _s5_discretize(a_re, a_im, log_step, b_re_t, b_im_t):
    n_in, groups, states = b_re_t.shape

    def body(are_ref, aim_ref, ls_ref, bre_ref, bim_ref, lre_ref, lim_ref, bbre_ref, bbim_ref):
        dt = jnp.exp(ls_ref[...])
        are, aim = are_ref[...], aim_ref[...]
        mag = jnp.exp(are * dt)
        lam_re = mag * jnp.cos(aim * dt)
        lam_im = mag * jnp.sin(aim * dt)
        den = are * are + aim * aim
        nr = lam_re - 1.0
        r_re = (nr * are + lam_im * aim) / den
        r_im = (lam_im * are - nr * aim) / den
        lre_ref[...] = lam_re
        lim_ref[...] = lam_im
        for i in range(n_in):
            bbre_ref[i] = r_re * bre_ref[i] - r_im * bim_ref[i]
            bbim_ref[i] = r_re * bim_ref[i] + r_im * bre_ref[i]

    gp = jax.ShapeDtypeStruct((groups, states), F32)
    igp = jax.ShapeDtypeStruct((n_in, groups, states), F32)
    return pl.pallas_call(body, out_shape=(gp, gp, igp, igp), name="s5_discretize",
                          compiler_params=pltpu.CompilerParams(vmem_limit_bytes=VMEM_LIMIT),
                          )(a_re, a_im, log_step, b_re_t, b_im_t)


def _s5_block_diag_in(bb_re_t, bb_im_t):
    n_in, groups, states = bb_re_t.shape
    gl = S5_GROUPS_PER_LANE_BLOCK
    eye = jnp.eye(gl, dtype=F32)

    def one(t):
        t = t.transpose(1, 0, 2).reshape(groups // gl, gl, n_in, states)
        return jnp.einsum('jgip,gh->jgihp', t, eye).reshape(groups // gl, gl * n_in, gl * states)

    return jnp.concatenate([one(bb_re_t), one(bb_im_t)], axis=-1).astype(BF16)


def _s5_block_diag_out(c_re, c_im):
    groups, n_in, states = c_re.shape
    gl = S5_GROUPS_PER_LANE_BLOCK
    eye = jnp.eye(gl, dtype=F32)

    def one(t):
        t = t.astype(F32).reshape(groups // gl, gl, n_in, states)
        return jnp.einsum('jgip,gh->jgphi', t, eye).reshape(groups // gl, gl * states, gl * n_in)

    return jnp.concatenate([one(c_re), -one(c_im)], axis=1).astype(BF16)


def _s5_scan(ug, bsz, seq, e, wb_f, wb_b, wc, lam_f, lam_b, d_skip, tb=128):
    nj = S5_LANE_BLOCKS_PER_STEP
    ns = S5_STATES_PER_LANE_BLOCK
    cw = nj * LANES
    nre = ns // LANES
    nlb = 2 * nre
    tb = _row_tile(seq, tb)
    nt = seq // tb
    n_cb = e // cw

    def body(uf_ref, ub_ref, wbf_ref, wbb_ref, wc_ref, lamf_ref, lamb_ref, d_ref,
             yf_ref, yb_ref, xf_ref, xb_ref, sf_ref, sb_ref):
        @pl.when(pl.program_id(2) == 0)
        def _():
            sf_ref[...] = jnp.zeros_like(sf_ref)
            sb_ref[...] = jnp.zeros_like(sb_ref)

        for j in range(nj):
            cols = slice(j * LANES, (j + 1) * LANES)
            bu_f = _dot(uf_ref[:, cols].astype(BF16), wbf_ref[j])
            bu_b = _dot(ub_ref[:, cols].astype(BF16), wbb_ref[j])
            for l in range(nlb):
                xf_ref[l, j * tb:(j + 1) * tb, :] = bu_f[:, l * LANES:(l + 1) * LANES]
                xb_ref[l, j * tb:(j + 1) * tb, :] = bu_b[:, l * LANES:(l + 1) * LANES]

        def lam_blocks(lam_ref):
            return [lam_ref[0, l // nre, :, (l % nre) * LANES:(l % nre + 1) * LANES] for l in range(nlb)]

        lam_f, lam_b = lam_blocks(lamf_ref), lam_blocks(lamb_ref)

        def advance(x_ref, rows, lam, state):
            new = []
            for l in range(nre):
                lr, li = lam[l], lam[nre + l]
                sr, si = state[l], state[nre + l]
                new.append((lr * sr - li * si + x_ref[l, rows, :], lr * si + li * sr + x_ref[nre + l, rows, :]))
            state = [p[0] for p in new] + [p[1] for p in new]
            for l in range(nlb):
                x_ref[l, rows, :] = state[l]
            return state

        def step(t, carry):
            sf, sb = carry
            sf = advance(xf_ref, pl.ds(t, nj, stride=tb), lam_f, sf)
            sb = advance(xb_ref, pl.ds(tb - 1 - t, nj, stride=tb), lam_b, sb)
            return sf, sb

        init = ([sf_ref[l] for l in range(nlb)], [sb_ref[l] for l in range(nlb)])
        sf, sb = lax.fori_loop(0, tb, step, init, unroll=8)
        for l in range(nlb):
            sf_ref[l] = sf[l]
            sb_ref[l] = sb[l]

        for j in range(nj):
            cols = slice(j * LANES, (j + 1) * LANES)
            rows = slice(j * tb, (j + 1) * tb)
            x_f = jnp.concatenate([xf_ref[l, rows, :] for l in range(nlb)], axis=-1).astype(BF16)
            x_b = jnp.concatenate([xb_ref[l, rows, :] for l in range(nlb)], axis=-1).astype(BF16)
            yf_ref[:, cols] = _dot(x_f, wc_ref[j]) + d_ref[:, cols] * uf_ref[:, cols]
            yb_ref[:, cols] = _dot(x_b, wc_ref[j])

    def fwd(b, c, t):
        return (b * nt + t, c)

    def bwd(b, c, t):
        return (b * nt + nt - 1 - t, c)

    w_in_spec = pl.BlockSpec((nj, LANES, 2 * ns), lambda b, c, t: (c, 0, 0))
    lam_spec = pl.BlockSpec((1, 2, nj, ns), lambda b, c, t: (c, 0, 0, 0))
    out = jax.ShapeDtypeStruct((bsz * seq, e), F32)
    return pl.pallas_call(
        body,
        grid=(bsz, n_cb, nt),
        in_specs=[pl.BlockSpec((tb, cw), fwd), pl.BlockSpec((tb, cw), bwd),
                  w_in_spec, w_in_spec,
                  pl.BlockSpec((nj, 2 * ns, LANES), lambda b, c, t: (c, 0, 0)),
                  lam_spec, lam_spec,
                  pl.BlockSpec((1, cw), lambda b, c, t: (0, c))],
        out_specs=(pl.BlockSpec((tb, cw), fwd), pl.BlockSpec((tb, cw), bwd)),
        out_shape=(out, out),
        scratch_shapes=[pltpu.VMEM((nlb, nj * tb, LANES), F32), pltpu.VMEM((nlb, nj * tb, LANES), F32),
                        pltpu.VMEM((nlb, nj, LANES), F32), pltpu.VMEM((nlb, nj, LANES), F32)],
        compiler_params=_params("parallel", "parallel", "arbitrary"),
        name="s5_scan",
    )(ug, ug, wb_f, wb_b, wc, lam_f, lam_b, d_skip.reshape(1, e))


S5_CHUNK = 64


def _s5_powers(a_re, a_im, log_step, b_re_t, b_im_t, n_pow):
    n_in, rows, lanes = b_re_t.shape

    def body(are_ref, aim_ref, ls_ref, bre_ref, bim_ref, pre_ref, pim_ref, bbre_ref, bbim_ref):
        dt = jnp.exp(ls_ref[...])
        are, aim = are_ref[...], aim_ref[...]
        mag = jnp.exp(are * dt)
        lam_re = mag * jnp.cos(aim * dt)
        lam_im = mag * jnp.sin(aim * dt)
        den = are * are + aim * aim
        nr = lam_re - 1.0
        r_re = (nr * are + lam_im * aim) / den
        r_im = (lam_im * are - nr * aim) / den
        for i in range(n_in):
            bbre_ref[i] = r_re * bre_ref[i] - r_im * bim_ref[i]
            bbim_ref[i] = r_re * bim_ref[i] + r_im * bre_ref[i]

        def power(tau, carry):
            t = tau.astype(F32)
            dtv = jnp.exp(ls_ref[...])
            m = jnp.exp(t * (are_ref[...] * dtv))
            ang = t * (aim_ref[...] * dtv)
            pre_ref[tau] = m * jnp.cos(ang)
            pim_ref[tau] = m * jnp.sin(ang)
            return carry

        lax.fori_loop(0, n_pow, power, 0)

    pw = jax.ShapeDtypeStruct((n_pow, rows, lanes), F32)
    bb = jax.ShapeDtypeStruct((n_in, rows, lanes), F32)
    return pl.pallas_call(body, out_shape=(pw, pw, bb, bb), name="s5_powers",
                          compiler_params=pltpu.CompilerParams(vmem_limit_bytes=VMEM_LIMIT),
                          )(a_re, a_im, log_step, b_re_t, b_im_t)


def _dot_nt_f32(a, b):
    a1, a2, a3 = _split3(a)
    b1, b2, b3 = _split3(b)
    return (_dot_nt(a1, b1) + (_dot_nt(a1, b2) + _dot_nt(a2, b1))
            + (_dot_nt(a1, b3) + _dot_nt(a2, b2) + _dot_nt(a3, b1)))


def _s5_tables(lr_in, li_in, xa, xb, lr_out, li_out, ca, cb,
               lhs_b, lhs_f, br2_f, bi2_f, br2_b, bi2_b, ca2, cb2):
    groups, tc, width = lr_in.shape
    n_in = xa.shape[1]

    def body(lri, lii, xa_r, xb_r, lro, lio, ca_r, cb_r, lb, lf, brf, bif, brb, bib, ca2_r, cb2_r,
             pin_ref, pout_ref, dtab_ref):
        for s in range(tc):
            rows = slice(s * n_in, (s + 1) * n_in)
            pin_ref[0, rows, :] = (lri[0, s:s + 1, :] * xa_r[0] + lii[0, s:s + 1, :] * xb_r[0]).astype(BF16)
            pout_ref[0, rows, :] = (lro[0, s:s + 1, :] * ca_r[0] + lio[0, s:s + 1, :] * cb_r[0]).astype(BF16)

        def gcat(br, bi):
            return jnp.concatenate(
                [br[0, j:j + 1, :] * ca2_r[0] + bi[0, j:j + 1, :] * cb2_r[0] for j in range(n_in)], axis=0)

        dtab_ref[0] = _dot_nt_f32(lb[0], gcat(brb, bib)) + _dot_nt_f32(lf[0], gcat(brf, bif))

    def spec(a):
        return pl.BlockSpec((1,) + a.shape[1:], lambda g: (g, 0, 0))

    ins = [lr_in, li_in, xa, xb, lr_out, li_out, ca, cb, lhs_b, lhs_f, br2_f, bi2_f, br2_b, bi2_b, ca2, cb2]
    pshape = jax.ShapeDtypeStruct((groups, tc * n_in, width), BF16)
    dshape = jax.ShapeDtypeStruct((groups, 2 * tc, n_in * n_in), F32)
    return pl.pallas_call(
        body, grid=(groups,), in_specs=[spec(a) for a in ins],
        out_specs=(spec(pshape), spec(pshape), spec(dshape)),
        out_shape=(pshape, pshape, dshape),
        compiler_params=_params("parallel"), name="s5_tables")(*ins)


def _s5_toeplitz(u, taps, p_in, p_out_t, mu, bsz):
    groups, rows, v = u.shape
    n_in = taps.shape[2]
    tc = v // n_in
    per_tile = LANES // n_in
    n_tiles = tc // per_tile
    nch = rows // bsz
    w4 = p_in.shape[2]
    half = w4 // 2

    def body(u_ref, t_ref, pin_ref, pout_ref, mu_ref, y_ref, m_ref, sl_ref, spf_ref, spb_ref):
        for s in range(tc):
            a, b = divmod(s, per_tile)
            lo = (n_tiles - 1 - a) * LANES
            m_ref[s * n_in:(s + 1) * n_in, :] = t_ref[0, b, :, lo:lo + v]
        uv = u_ref[0]
        sl_ref[...] = _dot(uv, pin_ref[0])
        mr, mi = mu_ref[0, 0:1, :], mu_ref[0, 1:2, :]

        sub = lax.broadcasted_iota(jnp.int32, (SUBLANES, half), 0)

        def step(k, carry):
            out = []
            for bi in range(bsz):
                fr, fi, br, bim = carry[bi]
                rows_f = pl.ds(pl.multiple_of(bi * nch + k * SUBLANES, SUBLANES), SUBLANES)
                rows_b = pl.ds(pl.multiple_of(bi * nch + nch - (k + 1) * SUBLANES, SUBLANES), SUBLANES)
                loc_f, loc_b = sl_ref[rows_f, :], sl_ref[rows_b, :]
                tfr = tfi = tbr = tbi = jnp.zeros((SUBLANES, half), F32)
                for i in range(SUBLANES):
                    ib = SUBLANES - 1 - i
                    tfr, tfi = jnp.where(sub == i, fr, tfr), jnp.where(sub == i, fi, tfi)
                    tbr, tbi = jnp.where(sub == ib, br, tbr), jnp.where(sub == ib, bim, tbi)
                    lfr, lfi = loc_f[i:i + 1, :half], loc_f[i:i + 1, half:]
                    lbr, lbi = loc_b[ib:ib + 1, :half], loc_b[ib:ib + 1, half:]
                    fr, fi = mr * fr - mi * fi + lfr, mr * fi + mi * fr + lfi
                    br, bim = mr * br - mi * bim + lbr, mr * bim + mi * br + lbi
                spf_ref[rows_f, :half] = tfr
                spf_ref[rows_f, half:] = tfi
                spb_ref[rows_b, :half] = tbr
                spb_ref[rows_b, half:] = tbi
                out.append((fr, fi, br, bim))
            return tuple(out)

        zero = jnp.zeros((1, half), F32)
        lax.fori_loop(0, nch // SUBLANES, step, tuple((zero, zero, zero, zero) for _ in range(bsz)))
        lane = lax.broadcasted_iota(jnp.int32, (rows, w4), 1)
        fwd_lane = (lane & (half - 1)) < (half // 2)
        carry_in = jnp.where(fwd_lane, spf_ref[...], spb_ref[...]).astype(BF16)
        y_ref[0] = _dot(uv, m_ref[...]) + _dot_nt(carry_in, pout_ref[0])

    def spec(a):
        nd = len(a.shape)
        return pl.BlockSpec((1,) + a.shape[1:], lambda g: (g,) + (0,) * (nd - 1))

    out = jax.ShapeDtypeStruct((groups, rows, v), F32)
    return pl.pallas_call(
        body, grid=(groups,),
        in_specs=[spec(u), spec(taps), spec(p_in), spec(p_out_t), spec(mu)],
        out_specs=spec(out), out_shape=out,
        scratch_shapes=[pltpu.VMEM((v, v), BF16), pltpu.VMEM((rows, w4), F32),
                        pltpu.VMEM((rows, w4), F32), pltpu.VMEM((rows, w4), F32)],
        compiler_params=_params("parallel"), name="s5_toeplitz")(u, taps, p_in, p_out_t, mu)


def _s5_mix(ug, bsz, seq, e, b_re, b_im, c_re, c_im, a_re_f, a_im_f, a_re_b, a_im_b, ls_f, ls_b):
    groups, states, n_in = b_re.shape
    tc = S5_CHUNK
    nch = seq // tc
    flat = lambda t: t.astype(F32).reshape(groups * states // LANES, LANES)
    bt = lambda t: t.astype(F32).transpose(2, 0, 1).reshape(n_in, groups * states // LANES, LANES)
    cat = lambda *ts: jnp.concatenate(ts, axis=-1)
    flip = lambda t: jnp.flip(t, axis=1)

    def direction(a_re, a_im, ls):
        pr, pi, bbr, bbi = _s5_powers(flat(a_re), flat(a_im), flat(ls), bt(b_re), bt(b_im), tc + 1)
        per_group = lambda t: t.reshape(t.shape[0], groups, states).transpose(1, 0, 2)
        return per_group(pr), per_group(pi), per_group(bbr), per_group(bbi)

    pfr, pfi, bfr, bfi = direction(a_re_f, a_im_f, ls_f)
    pbr, pbi, bbr, bbi = direction(a_re_b, a_im_b, ls_b)
    cr, ci = c_re.astype(F32), c_im.astype(F32)

    lr_in = cat(flip(pfr[:, :tc]), pbr[:, :tc], flip(pfr[:, :tc]), pbr[:, :tc])
    li_in = cat(flip(pfi[:, :tc]), pbi[:, :tc], flip(pfi[:, :tc]), pbi[:, :tc])
    lr_out = cat(pfr[:, 1:], flip(pbr[:, 1:]), pfr[:, 1:], flip(pbr[:, 1:]))
    li_out = cat(pfi[:, 1:], flip(pbi[:, 1:]), pfi[:, 1:], flip(pbi[:, 1:]))
    xa, xb = cat(bfr, bbr, bfi, bbi), cat(-bfi, -bbi, bfr, bbr)
    ca, cb = cat(cr, cr, -ci, -ci), cat(-ci, -ci, -cr, -cr)
    lhs_b = jnp.pad(cat(flip(pbr[:, :tc]), flip(pbi[:, :tc])), ((0, 0), (0, tc), (0, 0)))
    lhs_f = jnp.pad(cat(pfr[:, :tc], pfi[:, :tc]), ((0, 0), (tc - 1, 1), (0, 0)))
    ca2, cb2 = cat(cr, -ci), cat(-ci, -cr)
    p_in, p_out_t, dtab = _s5_tables(lr_in, li_in, xa, xb, lr_out, li_out, ca, cb, lhs_b, lhs_f,
                                     cat(bfr, bfr), cat(bfi, bfi), cat(bbr, bbr), cat(bbi, bbi), ca2, cb2)
    drow = dtab.reshape(groups, 2 * tc, n_in, n_in).transpose(0, 2, 1, 3).reshape(groups, n_in, 2 * tc * n_in)
    per_tile = LANES // n_in
    width = (2 * tc // per_tile - 1) * LANES
    taps = jnp.stack([drow[:, :, (per_tile - 1 - b) * n_in:(per_tile - 1 - b) * n_in + width]
                      for b in range(per_tile)], axis=1).astype(BF16)
    mu = jnp.stack([cat(pfr[:, tc], pbr[:, tc]), cat(pfi[:, tc], pbi[:, tc])], axis=1)

    u = ug[:, :e].reshape(bsz, nch, tc, groups, n_in).transpose(3, 0, 1, 2, 4)
    u = u.reshape(groups, bsz * nch, tc * n_in).astype(BF16)
    y = _s5_toeplitz(u, taps, p_in, p_out_t, mu, bsz)
    return y.reshape(groups, bsz, nch, tc, n_in).transpose(1, 2, 3, 0, 4).reshape(bsz * seq, e)


def _s5_glu(y, ug, d_skip, w_glu, b_glu, seq, tm=256, tn=1024):
    m, e = y.shape
    tm = _row_tile(seq, tm)
    tn = _col_tile(e, tn)
    g_col0 = e // tn

    def prologue(h_ref, y_ref, u_ref, d_ref):
        h_ref[...] = jax.nn.gelu(y_ref[...] + d_ref[...] * u_ref[...]).astype(BF16)

    def epilogue(acc, y_ref, u_ref, g_ref, d_ref, b_ref):
        z = jax.nn.gelu(y_ref[...] + d_ref[...] * u_ref[...])
        return z * jax.nn.sigmoid(acc + b_ref[...]) * _silu(g_ref[...])

    row_spec = pl.BlockSpec((tm, e), lambda i, j: (i, 0))
    tile_spec = pl.BlockSpec((tm, tn), lambda i, j: (i, j))
    vec_spec = pl.BlockSpec((1, tn), lambda i, j: (0, j))
    d2 = d_skip.reshape(1, e)
    return _fused_matmul(
        "s5_glu", [y, ug, d2], [row_spec, row_spec, pl.BlockSpec((1, e), lambda i, j: (0, 0))],
        prologue, w_glu,
        [y, ug, ug, d2, b_glu.reshape(1, e)],
        [tile_spec, tile_spec, pl.BlockSpec((tm, tn), lambda i, j: (i, g_col0 + j)), vec_spec, vec_spec],
        epilogue, m, e, e, tm, tn, BF16)


def _s5_layer(x2, bsz, seq, norm_w, scale, shift, gate, w_in, b_re, b_im, c_re, c_im, d_skip,
              a_re_f, a_im_f, a_re_b, a_im_b, log_step_f, log_step_b, w_glu, b_glu, w_out):
    e = w_out.shape[0]
    ug = _in_proj("s5_in", x2, seq, norm_w, scale, shift, w_in.astype(BF16), F32)
    y = _s5_mix(ug, bsz, seq, e, b_re, b_im, c_re, c_im, a_re_f, a_im_f, a_re_b, a_im_b,
                log_step_f, log_step_b)
    a = _s5_glu(y, ug, d_skip.astype(F32).reshape(e), w_glu.astype(BF16), b_glu.astype(F32), seq)
    return _out_proj("s5_out", a, w_out.astype(BF16), x2, gate, seq)


def _split3(x):
    p1 = x.astype(BF16)
    r1 = x - p1.astype(F32)
    p2 = r1.astype(BF16)
    p3 = (r1 - p2.astype(F32)).astype(BF16)
    return p1, p2, p3


def _gla_core(proj, lowrank, w2_f, bg_f, w2_b, bg_b, bsz, seq, e, chunks_per_step=4):
    heads = e // GLA_DV
    dk, dv, ch = GLA_DK, GLA_DV, GLA_CHUNK
    rt = min(chunks_per_step * ch, seq)
    ncs = rt // ch
    nb = seq // rt
    q_scale = dk ** -0.5

    def direction(forward, q_ref, k_ref, v_ref, lr_ref, w2_ref, bg_ref, o_ref, st_ref):
        row = lax.broadcasted_iota(jnp.int32, (rt, rt), 0)
        col = lax.broadcasted_iota(jnp.int32, (rt, rt), 1)
        shift = ch.bit_length() - 1
        same = (row >> shift) == (col >> shift)
        if forward:
            cum_mask = same & (col <= row)
            score_mask = cum_mask
        else:
            cum_mask = same & (col >= row)
            score_mask = same & (col > row)
        gk = _log_sigmoid(_dot(lr_ref[...].astype(BF16), w2_ref[...]) + bg_ref[...]) / GLA_GATE_NORM
        tri = jnp.where(cum_mask, 1.0, 0.0).astype(BF16)
        g1, g2, g3 = _split3(gk)
        bcum = _dot(tri, g1) + _dot(tri, g2) + _dot(tri, g3)
        q = q_ref[...] * q_scale
        k = k_ref[...]
        v = v_ref[...].astype(BF16)
        q_s = (q * jnp.exp(bcum)).astype(BF16)
        k_s = (k * jnp.exp(-bcum)).astype(BF16)
        scores = jnp.where(score_mask, _dot_nt(q_s, k_s), 0.0)
        o_intra = _dot(scores.astype(BF16), v)
        order = range(ncs) if forward else range(ncs - 1, -1, -1)
        for c in order:
            rows = slice(c * ch, (c + 1) * ch)
            last = c * ch + ch - 1 if forward else c * ch
            b_last = bcum[last:last + 1, :]
            state = st_ref[...]
            o_inter = _dot_nt(q_s[rows], state.astype(BF16))
            o_ref[rows, :] = o_intra[rows] + o_inter
            k_end = (k[rows] * jnp.exp(b_last - bcum[rows])).astype(BF16)
            st_ref[...] = state * jnp.exp(b_last) + _dot_tn(v[rows], k_end)

    def body(qf, kf, vf, lf, qb, kb, vb, lb, w2f, bgf, w2b, bgb, of, ob, stf, stb):
        @pl.when(pl.program_id(2) == 0)
        def _():
            stf[...] = jnp.zeros_like(stf)
            stb[...] = jnp.zeros_like(stb)

        direction(True, qf, kf, vf, lf, w2f, bgf, of, stf)
        direction(False, qb, kb, vb, lb, w2b, bgb, ob, stb)

    def tok(forward):
        return (lambda b, h, n: b * nb + n) if forward else (lambda b, h, n: b * nb + nb - 1 - n)

    def specs(forward):
        t = tok(forward)
        return [pl.BlockSpec((rt, dk), lambda b, h, n: (t(b, h, n), h)),
                pl.BlockSpec((rt, dk), lambda b, h, n: (t(b, h, n), heads + h)),
                pl.BlockSpec((rt, dv), lambda b, h, n: (t(b, h, n), heads + h)),
                pl.BlockSpec((rt, LANES), lambda b, h, n: (t(b, h, n), 0))]

    w2_spec = pl.BlockSpec((LANES, dk), lambda b, h, n: (0, h))
    bg_spec = pl.BlockSpec((1, dk), lambda b, h, n: (0, h))
    out = jax.ShapeDtypeStruct((bsz * seq, e), F32)
    tf, tbk = tok(True), tok(False)
    return pl.pallas_call(
        body,
        grid=(bsz, heads, nb),
        in_specs=specs(True) + specs(False) + [w2_spec, bg_spec, w2_spec, bg_spec],
        out_specs=(pl.BlockSpec((rt, dv), lambda b, h, n: (tf(b, h, n), h)),
                   pl.BlockSpec((rt, dv), lambda b, h, n: (tbk(b, h, n), h))),
        out_shape=(out, out),
        scratch_shapes=[pltpu.VMEM((dv, dk), F32), pltpu.VMEM((dv, dk), F32)],
        compiler_params=_params("parallel", "parallel", "arbitrary"),
        name="gla_core",
    )(proj, proj, proj, lowrank, proj, proj, proj, lowrank, w2_f, bg_f.reshape(1, -1),
      w2_b, bg_b.reshape(1, -1))


def _gla_out(of, ob, proj, norm_w, w_out, x2, gate, seq, tm=256, tn=1024):
    m, e = of.shape
    n = w_out.shape[1]
    heads = e // GLA_DV
    tm = _row_tile(seq, tm)
    tn = _col_tile(n, tn)
    g_block = (proj.shape[1] - e) // e

    def prologue(h_ref, of_ref, ob_ref, g_ref, nw_ref):
        for h in range(heads):
            cols = slice(h * GLA_DV, (h + 1) * GLA_DV)
            o = of_ref[:, cols] + ob_ref[:, cols]
            o = o * lax.rsqrt(jnp.mean(o * o, axis=-1, keepdims=True) + EPS) * nw_ref[...]
            h_ref[:, cols] = (o * _silu(g_ref[:, cols])).astype(BF16)

    row_spec = pl.BlockSpec((tm, e), lambda i, j: (i, 0))
    e_ins, e_specs = _residual_epilogue_specs(x2, gate, seq, tm, tn)
    return _fused_matmul(
        "gla_out", [of, ob, proj, norm_w.reshape(1, GLA_DV)],
        [row_spec, row_spec, pl.BlockSpec((tm, e), lambda i, j: (i, g_block)),
         pl.BlockSpec((1, GLA_DV), lambda i, j: (0, 0))],
        prologue, w_out, e_ins, e_specs, _residual_epilogue, m, e, n, tm, tn, F32)


def _gla_layer(x2, bsz, seq, norm_w, scale, shift, gate, w_in, w1_f, w2_f, bg_f, w1_b, w2_b, bg_b,
               gla_norm_w, w_out):
    e = w_out.shape[0]
    d = w_in.shape[0]
    rank = w1_f.shape[1]
    proj = _in_proj("gla_in", x2, seq, norm_w, scale, shift, w_in.astype(BF16), F32)
    w1 = jnp.zeros((d, LANES), BF16).at[:, :rank].set(w1_f.astype(BF16))
    w1 = w1.at[:, rank:2 * rank].set(w1_b.astype(BF16))
    lowrank = _in_proj("gla_in_gate_rank", x2, seq, norm_w, scale, shift, w1, F32, tn=LANES)
    kw = w2_f.shape[1]
    w2f = jnp.zeros((LANES, kw), BF16).at[:rank].set(w2_f.astype(BF16))
    w2b = jnp.zeros((LANES, kw), BF16).at[rank:2 * rank].set(w2_b.astype(BF16))
    of, ob = _gla_core(proj, lowrank, w2f, bg_f.astype(F32), w2b, bg_b.astype(F32), bsz, seq, e)
    return _gla_out(of, ob, proj, gla_norm_w.astype(F32), w_out.astype(BF16), x2, gate, seq)


def _final_norm(x2, w, tm=512):
    m, d = x2.shape
    tm = _row_tile(m, tm)

    def body(x_ref, w_ref, o_ref):
        xv = x_ref[...]
        o_ref[...] = xv * lax.rsqrt(jnp.mean(xv * xv, axis=-1, keepdims=True) + EPS) * w_ref[...]

    return pl.pallas_call(
        body, grid=(m // tm,),
        in_specs=[pl.BlockSpec((tm, d), lambda i: (i, 0)), pl.BlockSpec((1, d), lambda i: (0, 0))],
        out_specs=pl.BlockSpec((tm, d), lambda i: (i, 0)),
        out_shape=jax.ShapeDtypeStruct((m, d), F32),
        compiler_params=_params("parallel"),
        name="final_norm",
    )(x2, w.reshape(1, d).astype(F32))


def kernel(x, c, mod_w, mod_b, norm_w, na_w_in, na_rpb, na_w_out, s5_w_in, s5_b_re, s5_b_im, s5_c_re, s5_c_im, s5_d, s5_a_re_fwd, s5_a_im_fwd, s5_a_re_bwd, s5_a_im_bwd, s5_log_step_fwd, s5_log_step_bwd, s5_w_glu, s5_b_glu, s5_w_out, gla_w_in, gla_gk_w1_fwd, gla_gk_w2_fwd, gla_gk_b_fwd, gla_gk_w1_bwd, gla_gk_w2_bwd, gla_gk_b_bwd, gla_norm_w, gla_w_out, final_norm_w):
    bsz, seq, d = x.shape
    depth = mod_w.shape[0]
    mod = _modulation(c.astype(F32), mod_w, mod_b.astype(F32))
    x2 = x.astype(F32).reshape(bsz * seq, d)
    for i in range(depth):
        kind, j = i % 3, i // 3
        shift, scale, gate = mod[i, :, :d], mod[i, :, d:2 * d], mod[i, :, 2 * d:]
        nw = norm_w[i].astype(F32)
        if kind == 0:
            x2 = _na_layer(x2, bsz, seq, nw, scale, shift, gate, na_w_in[j], na_rpb[j], na_w_out[j])
        elif kind == 1:
            x2 = _s5_layer(x2, bsz, seq, nw, scale, shift, gate, s5_w_in[j], s5_b_re[j], s5_b_im[j],
                           s5_c_re[j], s5_c_im[j], s5_d[j], s5_a_re_fwd[j], s5_a_im_fwd[j],
                           s5_a_re_bwd[j], s5_a_im_bwd[j], s5_log_step_fwd[j], s5_log_step_bwd[j],
                           s5_w_glu[j], s5_b_glu[j], s5_w_out[j])
        else:
            x2 = _gla_layer(x2, bsz, seq, nw, scale, shift, gate, gla_w_in[j], gla_gk_w1_fwd[j],
                            gla_gk_w2_fwd[j], gla_gk_b_fwd[j], gla_gk_w1_bwd[j], gla_gk_w2_bwd[j],
                            gla_gk_b_bwd[j], gla_norm_w[j], gla_w_out[j])
    return _final_norm(x2, final_norm_w).reshape(bsz, seq, d)
```

```python
import jax
import jax.numpy as jnp
from jax import lax
from jax.experimental import pallas as pl
from jax.experimental.pallas import tpu as pltpu

EPS = 1e-6
GRID_W = 64
NA_HEAD_DIM = 128
NA_ROWS = 8
NA_COLS = 16
NA_ROW_BLOCK = 8
NA_HEADS_PER_STEP = 2
NA_PIPELINE_GROUP = 4
S5_GROUP = 16
S5_STATE = 64
GLA_DV = 512
GLA_DK = 256
GLA_GATE_NORM = 16.0
GLA_CHUNK = 64
NEG_INF = -1e30

V7X_VMEM_BYTES = 64 * 1024 * 1024
VMEM_LIMIT = V7X_VMEM_BYTES - 8 * 1024 * 1024
LANES = 128
SUBLANES = 8
BF16 = jnp.bfloat16
F32 = jnp.float32


def _params(*sem):
    return pltpu.CompilerParams(dimension_semantics=sem, vmem_limit_bytes=VMEM_LIMIT)


def _silu(x):
    return x * jax.nn.sigmoid(x)


def _log_sigmoid(x):
    return jnp.minimum(x, 0.0) - jnp.log1p(jnp.exp(-jnp.abs(x)))


def _dot(a, b):
    return jnp.dot(a, b, preferred_element_type=F32)


def _dot_nt(a, b):
    return lax.dot_general(a, b, (((1,), (1,)), ((), ())), preferred_element_type=F32)


def _dot_tn(a, b):
    return lax.dot_general(a, b, (((0,), (0,)), ((), ())), preferred_element_type=F32)


def _split3(x):
    p1 = x.astype(BF16)
    r1 = x - p1.astype(F32)
    p2 = r1.astype(BF16)
    p3 = (r1 - p2.astype(F32)).astype(BF16)
    return p1, p2, p3


def _dot_nt_f32(a, b):
    a1, a2, a3 = _split3(a)
    b1, b2, b3 = _split3(b)
    return (_dot_nt(a1, b1) + (_dot_nt(a1, b2) + _dot_nt(a2, b1))
            + (_dot_nt(a1, b3) + _dot_nt(a2, b2) + _dot_nt(a3, b1)))


def _tile(n, want):
    t = min(want, n)
    assert n % t == 0
    return t


def _matmul(name, a, w, layer, col0, n, e_ins, e_specs, epilogue, out_dtype, seq, tm=1024, tn=512):
    m, k = a.shape
    tm = _tile(seq, tm)
    tn = _tile(n, tn)
    assert col0 % tn == 0
    jb = col0 // tn
    n_e = len(e_ins)

    def body(a_ref, w_ref, *rest):
        e_refs, o_ref, wb_ref = rest[:n_e], rest[n_e], rest[n_e + 1]

        @pl.when(pl.program_id(1) == 0)
        def _():
            wb_ref[...] = w_ref[...].astype(BF16)

        o_ref[...] = epilogue(_dot(a_ref[...], wb_ref[...]), *e_refs).astype(out_dtype)

    return pl.pallas_call(
        body,
        grid=(n // tn, m // tm),
        in_specs=[pl.BlockSpec((tm, k), lambda j, i: (i, 0)),
                  pl.BlockSpec((None, k, tn), lambda j, i: (layer, 0, jb + j))] + list(e_specs(tm, tn)),
        out_specs=pl.BlockSpec((tm, tn), lambda j, i: (i, j)),
        out_shape=jax.ShapeDtypeStruct((m, n), out_dtype),
        scratch_shapes=[pltpu.VMEM((k, tn), BF16)],
        compiler_params=_params("parallel", "arbitrary"),
        name=name,
    )(a, w, *e_ins)


def _no_specs(tm, tn):
    return []


def _identity(acc):
    return acc


def _out_proj(name, a, w, layer, x2, gate, seq):
    bsz, d = gate.shape

    def specs(tm, tn):
        tiles_per_seq = seq // tm
        return [pl.BlockSpec((tm, tn), lambda j, i: (i, j)),
                pl.BlockSpec((1, 1, tn), lambda j, i: (i // tiles_per_seq, 0, j))]

    def epilogue(acc, x_ref, gate_ref):
        return x_ref[...] + gate_ref[0] * acc

    return _matmul(name, a, w, layer, 0, d, [x2, gate.reshape(bsz, 1, d)], specs, epilogue, F32, seq)


def _modulation(c, mod_w, mod_b):
    depth, d, n = mod_w.shape
    bsz = c.shape[0]
    c_pad = jnp.zeros((SUBLANES, d), F32).at[:bsz].set(c)
    tn = _tile(n, 512)

    def body(c_ref, w_ref, b_ref, o_ref):
        act = _silu(c_ref[...]).astype(BF16)
        o_ref[0] = _dot(act, w_ref[0].astype(BF16)) + b_ref[0]

    out = pl.pallas_call(
        body,
        grid=(depth, n // tn),
        in_specs=[pl.BlockSpec((SUBLANES, d), lambda i, j: (0, 0)),
                  pl.BlockSpec((1, d, tn), lambda i, j: (i, 0, j)),
                  pl.BlockSpec((1, 1, tn), lambda i, j: (i, 0, j))],
        out_specs=pl.BlockSpec((1, SUBLANES, tn), lambda i, j: (i, 0, j)),
        out_shape=jax.ShapeDtypeStruct((depth, SUBLANES, n), F32),
        compiler_params=_params("parallel", "parallel"),
        name="modulation",
    )(c_pad, mod_w, mod_b.reshape(depth, 1, n))
    return out[:, :bsz]


def _norm_modulate(x2, seq, norm_w, scale, shift, tm=512):
    m, d = x2.shape
    tm = _tile(seq, tm)
    tiles_per_seq = seq // tm
    bsz = m // seq

    def body(x_ref, nw_ref, sc_ref, sh_ref, o_ref):
        xv = x_ref[...]
        y = xv * lax.rsqrt(jnp.mean(xv * xv, axis=-1, keepdims=True) + EPS) * nw_ref[...]
        o_ref[...] = (y * (1.0 + sc_ref[0]) + sh_ref[0]).astype(BF16)

    vec_spec = pl.BlockSpec((1, 1, d), lambda i: (i // tiles_per_seq, 0, 0))
    return pl.pallas_call(
        body, grid=(m // tm,),
        in_specs=[pl.BlockSpec((tm, d), lambda i: (i, 0)), pl.BlockSpec((1, d), lambda i: (0, 0)),
                  vec_spec, vec_spec],
        out_specs=pl.BlockSpec((tm, d), lambda i: (i, 0)),
        out_shape=jax.ShapeDtypeStruct((m, d), BF16),
        compiler_params=_params("parallel"),
        name="norm_modulate",
    )(x2, norm_w.reshape(1, d), scale.reshape(bsz, 1, d), shift.reshape(bsz, 1, d))


def _na_bias_table(rpb):
    w = GRID_W
    qc = jnp.arange(w)
    kc = jnp.arange(w)
    col_start = jnp.clip(qc - NA_COLS // 2, 0, w - NA_COLS)
    col_ok = (kc[None, :] >= col_start[:, None]) & (kc[None, :] < col_start[:, None] + NA_COLS)
    dc = kc[None, :] - qc[:, None] + NA_COLS - 1
    onehot = ((dc[None] == jnp.arange(2 * NA_COLS - 1)[:, None, None]) & col_ok[None]).astype(F32)
    cexp = jnp.einsum('hdc,cqk->hdqk', rpb.astype(F32), onehot, precision=lax.Precision.HIGHEST)
    cexp = jnp.where(col_ok[None, None], cexp, NEG_INF)
    tables = [jnp.concatenate([cexp[:, t + m] for m in range(NA_ROWS)], axis=-1) for t in range(NA_ROWS)]
    return jnp.stack(tables, axis=1)


def _na_attention(qkv, g, bias, bsz, seq):
    e = g.shape[1]
    heads = e // NA_HEAD_DIM
    rows = seq // GRID_W
    tq = NA_ROW_BLOCK * GRID_W
    nblk = rows // NA_ROW_BLOCK
    assert nblk >= 2 and rows % NA_ROW_BLOCK == 0
    half = tq // 2
    hp = NA_HEADS_PER_STEP
    hw = hp * NA_HEAD_DIM
    win = NA_ROWS * GRID_W
    half_rows = NA_ROWS // 2
    scale = NA_HEAD_DIM ** -0.5
    assert heads % hp == 0 and NA_ROW_BLOCK == NA_ROWS

    def body(q_ref, kp_ref, kc_ref, kn_ref, vp_ref, vc_ref, vn_ref, g_ref, b_ref, o_ref,
             kcat_ref, vcat_ref):
        kcat_ref[:half] = kp_ref[...]
        kcat_ref[half:half + tq] = kc_ref[...]
        kcat_ref[half + tq:] = kn_ref[...]
        vcat_ref[:half] = vp_ref[...]
        vcat_ref[half:half + tq] = vc_ref[...]
        vcat_ref[half + tq:] = vn_ref[...]

        def attend(r0):
            def place(qr, h):
                r = r0 + qr
                row_start = min(max(r - half_rows, 0), rows - NA_ROWS)
                start = (row_start - r0 + half_rows) * GRID_W
                return (slice(qr * GRID_W, (qr + 1) * GRID_W), slice(h * NA_HEAD_DIM, (h + 1) * NA_HEAD_DIM),
                        slice(start, start + win), row_start - r + NA_ROWS - 1)

            def scores(qr, h):
                q_rows, lanes, keys, table = place(qr, h)
                return _dot_nt(q_ref[q_rows, lanes], kcat_ref[keys, lanes]) * scale + b_ref[h, table]

            def finish(qr, h, s):
                q_rows, lanes, keys, _ = place(qr, h)
                p = jnp.exp(s - jnp.max(s, axis=-1, keepdims=True))
                denom = jnp.sum(p, axis=-1, keepdims=True)
                o = _dot(p.astype(BF16), vcat_ref[keys, lanes]) / denom
                o_ref[q_rows, lanes] = (o * _silu(g_ref[q_rows, lanes])).astype(BF16)

            work = [(qr, h) for qr in range(NA_ROW_BLOCK) for h in range(hp)]
            groups = [work[i:i + NA_PIPELINE_GROUP] for i in range(0, len(work), NA_PIPELINE_GROUP)]
            pending = [scores(*w) for w in groups[0]]
            for gi, grp in enumerate(groups):
                ahead = [scores(*w) for w in groups[gi + 1]] if gi + 1 < len(groups) else []
                for w, s in zip(grp, pending):
                    finish(*w, s)
                pending = ahead

        blk = pl.program_id(2)
        pl.when(blk == 0)(lambda: attend(0))
        pl.when(blk == nblk - 1)(lambda: attend(rows - NA_ROW_BLOCK))
        pl.when((blk > 0) & (blk < nblk - 1))(lambda: attend(NA_ROW_BLOCK))

    def cur(col0):
        return pl.BlockSpec((tq, hw), lambda b, h, i: (b * nblk + i, col0 + h))

    def above(col0):
        return pl.BlockSpec((half, hw), lambda b, h, i: (jnp.maximum(2 * (b * nblk + i) - 1, 0), col0 + h))

    def below(col0):
        last = 2 * bsz * nblk - 1
        return pl.BlockSpec((half, hw), lambda b, h, i: (jnp.minimum(2 * (b * nblk + i) + 2, last), col0 + h))

    kcol, vcol = heads // hp, 2 * heads // hp
    return pl.pallas_call(
        body,
        grid=(bsz, heads // hp, nblk),
        in_specs=[cur(0),
                  above(kcol), cur(kcol), below(kcol),
                  above(vcol), cur(vcol), below(vcol),
                  cur(0),
                  pl.BlockSpec((hp, NA_ROWS, GRID_W, win), lambda b, h, i: (h, 0, 0, 0))],
        out_specs=cur(0),
        out_shape=jax.ShapeDtypeStruct((bsz * seq, e), BF16),
        scratch_shapes=[pltpu.VMEM((2 * tq, hw), BF16), pltpu.VMEM((2 * tq, hw), BF16)],
        compiler_params=_params("parallel", "parallel", "arbitrary"),
        name="na_attention",
    )(qkv, qkv, qkv, qkv, qkv, qkv, qkv, g, bias)


def _na_layer(x2, h, layer, bsz, seq, gate, w_in, rpb, w_out):
    e = w_out.shape[1]
    qkv = _matmul("na_in_qkv", h, w_in, layer, 0, 3 * e, [], _no_specs, _identity, BF16, seq)
    g = _matmul("na_in_gate", h, w_in, layer, 3 * e, e, [], _no_specs, _identity, F32, seq)
    a = _na_attention(qkv, g, _na_bias_table(rpb[layer]), bsz, seq)
    return _out_proj("na_out", a, w_out, layer, x2, gate, seq)


S5_CHUNK = 64


def _s5_powers(a_re, a_im, log_step, b_re_t, b_im_t, n_pow):
    n_in, rows, lanes = b_re_t.shape

    def body(are_ref, aim_ref, ls_ref, bre_ref, bim_ref, pre_ref, pim_ref, bbre_ref, bbim_ref):
        dt = jnp.exp(ls_ref[...])
        are, aim = are_ref[...], aim_ref[...]
        mag = jnp.exp(are * dt)
        lam_re = mag * jnp.cos(aim * dt)
        lam_im = mag * jnp.sin(aim * dt)
        den = are * are + aim * aim
        nr = lam_re - 1.0
        r_re = (nr * are + lam_im * aim) / den
        r_im = (lam_im * are - nr * aim) / den
        for i in range(n_in):
            bbre_ref[i] = r_re * bre_ref[i] - r_im * bim_ref[i]
            bbim_ref[i] = r_re * bim_ref[i] + r_im * bre_ref[i]

        def power(tau, carry):
            t = jnp.asarray(tau, F32)
            dtv = jnp.exp(ls_ref[...])
            m = jnp.exp(t * (are_ref[...] * dtv))
            ang = t * (aim_ref[...] * dtv)
            pre_ref[tau] = m * jnp.cos(ang)
            pim_ref[tau] = m * jnp.sin(ang)
            return carry

        lax.fori_loop(0, n_pow, power, 0)

    pw = jax.ShapeDtypeStruct((n_pow, rows, lanes), F32)
    bb = jax.ShapeDtypeStruct((n_in, rows, lanes), F32)
    return pl.pallas_call(body, out_shape=(pw, pw, bb, bb), name="s5_powers",
                          compiler_params=pltpu.CompilerParams(vmem_limit_bytes=VMEM_LIMIT),
                          )(a_re, a_im, log_step, b_re_t, b_im_t)


def _s5_tables(lr_in, li_in, xa, xb, lr_out, li_out, ca, cb,
               lhs_b, lhs_f, br2_f, bi2_f, br2_b, bi2_b, ca2, cb2):
    groups, tc, width = lr_in.shape
    n_in = xa.shape[1]

    def body(lri, lii, xa_r, xb_r, lro, lio, ca_r, cb_r, lb, lf, brf, bif, brb, bib, ca2_r, cb2_r,
             pin_ref, pout_ref, dtab_ref):
        for s in range(tc):
            rows = slice(s * n_in, (s + 1) * n_in)
            pin_ref[0, rows, :] = (lri[0, s:s + 1, :] * xa_r[0] + lii[0, s:s + 1, :] * xb_r[0]).astype(BF16)
            pout_ref[0, rows, :] = (lro[0, s:s + 1, :] * ca_r[0] + lio[0, s:s + 1, :] * cb_r[0]).astype(BF16)

        def gcat(br, bi):
            return jnp.concatenate(
                [br[0, j:j + 1, :] * ca2_r[0] + bi[0, j:j + 1, :] * cb2_r[0] for j in range(n_in)], axis=0)

        dtab_ref[0] = _dot_nt_f32(lb[0], gcat(brb, bib)) + _dot_nt_f32(lf[0], gcat(brf, bif))

    def spec(a):
        return pl.BlockSpec((1,) + a.shape[1:], lambda g: (g, 0, 0))

    ins = [lr_in, li_in, xa, xb, lr_out, li_out, ca, cb, lhs_b, lhs_f, br2_f, bi2_f, br2_b, bi2_b, ca2, cb2]
    pshape = jax.ShapeDtypeStruct((groups, tc * n_in, width), BF16)
    dshape = jax.ShapeDtypeStruct((groups, 2 * tc, n_in * n_in), F32)
    return pl.pallas_call(
        body, grid=(groups,), in_specs=[spec(a) for a in ins],
        out_specs=(spec(pshape), spec(pshape), spec(dshape)),
        out_shape=(pshape, pshape, dshape),
        compiler_params=_params("parallel"), name="s5_tables")(*ins)


def _s5_toeplitz(u, taps, p_in, p_out_t, mu, bsz):
    groups, rows, v = u.shape
    n_in = taps.shape[2]
    tc = v // n_in
    per_tile = LANES // n_in
    n_tiles = tc // per_tile
    nch = rows // bsz
    w4 = p_in.shape[2]
    half = w4 // 2

    def body(u_ref, t_ref, pin_ref, pout_ref, mu_ref, y_ref, m_ref, sl_ref, spf_ref, spb_ref):
        for s in range(tc):
            a, b = divmod(s, per_tile)
            lo = (n_tiles - 1 - a) * LANES
            m_ref[s * n_in:(s + 1) * n_in, :] = t_ref[0, b, :, lo:lo + v]
        uv = u_ref[0]
        sl_ref[...] = _dot(uv, pin_ref[0])
        mr, mi = mu_ref[0, 0:1, :], mu_ref[0, 1:2, :]
        sub = lax.broadcasted_iota(jnp.int32, (SUBLANES, half), 0)

        def step(k, carry):
            out = []
            for bi in range(bsz):
                fr, fi, br, bim = carry[bi]
                rows_f = pl.ds(pl.multiple_of(bi * nch + k * SUBLANES, SUBLANES), SUBLANES)
                rows_b = pl.ds(pl.multiple_of(bi * nch + nch - (k + 1) * SUBLANES, SUBLANES), SUBLANES)
                loc_f, loc_b = sl_ref[rows_f, :], sl_ref[rows_b, :]
                tfr = tfi = tbr = tbi = jnp.zeros((SUBLANES, half), F32)
                for i in range(SUBLANES):
                    ib = SUBLANES - 1 - i
                    tfr, tfi = jnp.where(sub == i, fr, tfr), jnp.where(sub == i, fi, tfi)
                    tbr, tbi = jnp.where(sub == ib, br, tbr), jnp.where(sub == ib, bim, tbi)
                    lfr, lfi = loc_f[i:i + 1, :half], loc_f[i:i + 1, half:]
                    lbr, lbi = loc_b[ib:ib + 1, :half], loc_b[ib:ib + 1, half:]
                    fr, fi = mr * fr - mi * fi + lfr, mr * fi + mi * fr + lfi
                    br, bim = mr * br - mi * bim + lbr, mr * bim + mi * br + lbi
                spf_ref[rows_f, :half] = tfr
                spf_ref[rows_f, half:] = tfi
                spb_ref[rows_b, :half] = tbr
                spb_ref[rows_b, half:] = tbi
                out.append((fr, fi, br, bim))
            return tuple(out)

        zero = jnp.zeros((1, half), F32)
        lax.fori_loop(0, nch // SUBLANES, step, tuple((zero, zero, zero, zero) for _ in range(bsz)))
        lane = lax.broadcasted_iota(jnp.int32, (rows, w4), 1)
        fwd_lane = (lane & (half - 1)) < (half // 2)
        carry_in = jnp.where(fwd_lane, spf_ref[...], spb_ref[...]).astype(BF16)
        y_ref[0] = _dot(uv, m_ref[...]) + _dot_nt(carry_in, pout_ref[0])

    def spec(a):
        nd = len(a.shape)
        return pl.BlockSpec((1,) + a.shape[1:], lambda g: (g,) + (0,) * (nd - 1))

    out = jax.ShapeDtypeStruct((groups, rows, v), F32)
    return pl.pallas_call(
        body, grid=(groups,),
        in_specs=[spec(u), spec(taps), spec(p_in), spec(p_out_t), spec(mu)],
        out_specs=spec(out), out_shape=out,
        scratch_shapes=[pltpu.VMEM((v, v), BF16), pltpu.VMEM((rows, w4), F32),
                        pltpu.VMEM((rows, w4), F32), pltpu.VMEM((rows, w4), F32)],
        compiler_params=_params("parallel"), name="s5_toeplitz")(u, taps, p_in, p_out_t, mu)


def _s5_mix(ug, bsz, seq, e, b_re, b_im, c_re, c_im, a_re_f, a_im_f, a_re_b, a_im_b, ls_f, ls_b):
    groups, states, n_in = b_re.shape
    tc = S5_CHUNK
    nch = seq // tc
    flat = lambda t: t.astype(F32).reshape(groups * states // LANES, LANES)
    bt = lambda t: t.astype(F32).transpose(2, 0, 1).reshape(n_in, groups * states // LANES, LANES)
    cat = lambda *ts: jnp.concatenate(ts, axis=-1)
    flip = lambda t: jnp.flip(t, axis=1)

    def direction(a_re, a_im, ls):
        pr, pi, bbr, bbi = _s5_powers(flat(a_re), flat(a_im), flat(ls), bt(b_re), bt(b_im), tc + 1)
        per_group = lambda t: t.reshape(t.shape[0], groups, states).transpose(1, 0, 2)
        return per_group(pr), per_group(pi), per_group(bbr), per_group(bbi)

    pfr, pfi, bfr, bfi = direction(a_re_f, a_im_f, ls_f)
    pbr, pbi, bbr, bbi = direction(a_re_b, a_im_b, ls_b)
    cr, ci = c_re.astype(F32), c_im.astype(F32)

    lr_in = cat(flip(pfr[:, :tc]), pbr[:, :tc], flip(pfr[:, :tc]), pbr[:, :tc])
    li_in = cat(flip(pfi[:, :tc]), pbi[:, :tc], flip(pfi[:, :tc]), pbi[:, :tc])
    lr_out = cat(pfr[:, 1:], flip(pbr[:, 1:]), pfr[:, 1:], flip(pbr[:, 1:]))
    li_out = cat(pfi[:, 1:], flip(pbi[:, 1:]), pfi[:, 1:], flip(pbi[:, 1:]))
    xa, xb = cat(bfr, bbr, bfi, bbi), cat(-bfi, -bbi, bfr, bbr)
    ca, cb = cat(cr, cr, -ci, -ci), cat(-ci, -ci, -cr, -cr)
    lhs_b = jnp.pad(cat(flip(pbr[:, :tc]), flip(pbi[:, :tc])), ((0, 0), (0, tc), (0, 0)))
    lhs_f = jnp.pad(cat(pfr[:, :tc], pfi[:, :tc]), ((0, 0), (tc - 1, 1), (0, 0)))
    ca2, cb2 = cat(cr, -ci), cat(-ci, -cr)
    p_in, p_out_t, dtab = _s5_tables(lr_in, li_in, xa, xb, lr_out, li_out, ca, cb, lhs_b, lhs_f,
                                     cat(bfr, bfr), cat(bfi, bfi), cat(bbr, bbr), cat(bbi, bbi), ca2, cb2)
    drow = dtab.reshape(groups, 2 * tc, n_in, n_in).transpose(0, 2, 1, 3).reshape(groups, n_in, 2 * tc * n_in)
    per_tile = LANES // n_in
    width = (2 * tc // per_tile - 1) * LANES
    taps = jnp.stack([drow[:, :, (per_tile - 1 - b) * n_in:(per_tile - 1 - b) * n_in + width]
                      for b in range(per_tile)], axis=1).astype(BF16)
    mu = jnp.stack([cat(pfr[:, tc], pbr[:, tc]), cat(pfi[:, tc], pbi[:, tc])], axis=1)

    u = ug[:, :e].reshape(bsz, nch, tc, groups, n_in).transpose(3, 0, 1, 2, 4)
    u = u.reshape(groups, bsz * nch, tc * n_in).astype(BF16)
    y = _s5_toeplitz(u, taps, p_in, p_out_t, mu, bsz)
    return y.reshape(groups, bsz, nch, tc, n_in).transpose(1, 2, 3, 0, 4).reshape(bsz * seq, e)


def _s5_gelu(y, ug, d_skip, tm=512, tn=1024):
    m, e = y.shape
    tm, tn = _tile(m, tm), _tile(e, tn)

    def body(y_ref, u_ref, d_ref, o_ref):
        o_ref[...] = jax.nn.gelu(y_ref[...] + d_ref[...] * u_ref[...]).astype(BF16)

    tile = pl.BlockSpec((tm, tn), lambda i, j: (i, j))
    return pl.pallas_call(
        body, grid=(m // tm, e // tn),
        in_specs=[tile, tile, pl.BlockSpec((1, tn), lambda i, j: (0, j))],
        out_specs=tile, out_shape=jax.ShapeDtypeStruct((m, e), BF16),
        compiler_params=_params("parallel", "parallel"), name="s5_gelu")(y, ug, d_skip.reshape(1, e))


def _s5_glu(z, y, ug, d_skip, w_glu, layer, b_glu, seq):
    e = y.shape[1]

    def specs(tm, tn):
        tile = pl.BlockSpec((tm, tn), lambda j, i: (i, j))
        vec = pl.BlockSpec((1, tn), lambda j, i: (0, j))
        return [tile, tile, pl.BlockSpec((tm, tn), lambda j, i: (i, e // tn + j)), vec, vec]

    def epilogue(acc, y_ref, u_ref, g_ref, d_ref, b_ref):
        zf = jax.nn.gelu(y_ref[...] + d_ref[...] * u_ref[...])
        return zf * jax.nn.sigmoid(acc + b_ref[...]) * _silu(g_ref[...])

    return _matmul("s5_glu", z, w_glu, layer, 0, e, [y, ug, ug, d_skip.reshape(1, e), b_glu.reshape(1, e)],
                   specs, epilogue, BF16, seq, tm=512)


def _s5_layer(x2, h, layer, bsz, seq, gate, w_in, b_re, b_im, c_re, c_im, d_skip,
              a_re_f, a_im_f, a_re_b, a_im_b, log_step_f, log_step_b, w_glu, b_glu, w_out):
    e = w_out.shape[1]
    ug = _matmul("s5_in", h, w_in, layer, 0, 2 * e, [], _no_specs, _identity, F32, seq)
    pick = lambda t: t[layer]
    y = _s5_mix(ug, bsz, seq, e, pick(b_re), pick(b_im), pick(c_re), pick(c_im), pick(a_re_f), pick(a_im_f),
                pick(a_re_b), pick(a_im_b), pick(log_step_f), pick(log_step_b))
    d = d_skip[layer].astype(F32).reshape(e)
    z = _s5_gelu(y, ug, d)
    a = _s5_glu(z, y, ug, d, w_glu, layer, b_glu[layer].astype(F32), seq)
    return _out_proj("s5_out", a, w_out, layer, x2, gate, seq)


def _gla_core(proj, lowrank, w2_f, bg_f, w2_b, bg_b, bsz, seq, e, chunks_per_step=4):
    heads = e // GLA_DV
    dk, dv, ch = GLA_DK, GLA_DV, GLA_CHUNK
    rt = min(chunks_per_step * ch, seq)
    ncs = rt // ch
    nb = seq // rt
    q_scale = dk ** -0.5

    def prepare(forward, q_ref, k_ref, v_ref, lr_ref, w2_ref, bg_ref):
        row = lax.broadcasted_iota(jnp.int32, (rt, rt), 0)
        col = lax.broadcasted_iota(jnp.int32, (rt, rt), 1)
        shift = ch.bit_length() - 1
        same = (row >> shift) == (col >> shift)
        if forward:
            cum_mask = same & (col <= row)
            score_mask = cum_mask
        else:
            cum_mask = same & (col >= row)
            score_mask = same & (col > row)
        gk = _log_sigmoid(_dot(lr_ref[...].astype(BF16), w2_ref[...]) + bg_ref[...]) / GLA_GATE_NORM
        tri = jnp.where(cum_mask, 1.0, 0.0).astype(BF16)
        g1, g2, g3 = _split3(gk)
        bcum = _dot(tri, g1) + _dot(tri, g2) + _dot(tri, g3)
        k = k_ref[...]
        v = v_ref[...].astype(BF16)
        q_s = (q_ref[...] * q_scale * jnp.exp(bcum)).astype(BF16)
        k_s = (k * jnp.exp(-bcum)).astype(BF16)
        scores = jnp.where(score_mask, _dot_nt(q_s, k_s), 0.0)
        o_intra = _dot(scores.astype(BF16), v)
        return q_s, k, v, bcum, o_intra

    def advance(forward, c, prep, o_ref, st_ref):
        q_s, k, v, bcum, o_intra = prep
        rows = slice(c * ch, (c + 1) * ch)
        last = c * ch + ch - 1 if forward else c * ch
        b_last = bcum[last:last + 1, :]
        state = st_ref[...]
        o_ref[rows, :] = o_intra[rows] + _dot_nt(q_s[rows], state.astype(BF16))
        k_end = (k[rows] * jnp.exp(b_last - bcum[rows])).astype(BF16)
        st_ref[...] = state * jnp.exp(b_last) + _dot_tn(v[rows], k_end)

    def body(qf, kf, vf, lf, qb, kb, vb, lb, w2f, bgf, w2b, bgb, of, ob, stf, stb):
        @pl.when(pl.program_id(2) == 0)
        def _():
            stf[...] = jnp.zeros_like(stf)
            stb[...] = jnp.zeros_like(stb)

        prep_f = prepare(True, qf, kf, vf, lf, w2f, bgf)
        prep_b = prepare(False, qb, kb, vb, lb, w2b, bgb)
        for c in range(ncs):
            advance(True, c, prep_f, of, stf)
            advance(False, ncs - 1 - c, prep_b, ob, stb)

    def tok(forward):
        return (lambda b, h, n: b * nb + n) if forward else (lambda b, h, n: b * nb + nb - 1 - n)

    def specs(forward):
        t = tok(forward)
        return [pl.BlockSpec((rt, dk), lambda b, h, n: (t(b, h, n), h)),
                pl.BlockSpec((rt, dk), lambda b, h, n: (t(b, h, n), heads + h)),
                pl.BlockSpec((rt, dv), lambda b, h, n: (t(b, h, n), heads + h)),
                pl.BlockSpec((rt, LANES), lambda b, h, n: (t(b, h, n), 0))]

    w2_spec = pl.BlockSpec((LANES, dk), lambda b, h, n: (0, h))
    bg_spec = pl.BlockSpec((1, dk), lambda b, h, n: (0, h))
    out = jax.ShapeDtypeStruct((bsz * seq, e), F32)
    tf, tbk = tok(True), tok(False)
    return pl.pallas_call(
        body,
        grid=(bsz, heads, nb),
        in_specs=specs(True) + specs(False) + [w2_spec, bg_spec, w2_spec, bg_spec],
        out_specs=(pl.BlockSpec((rt, dv), lambda b, h, n: (tf(b, h, n), h)),
                   pl.BlockSpec((rt, dv), lambda b, h, n: (tbk(b, h, n), h))),
        out_shape=(out, out),
        scratch_shapes=[pltpu.VMEM((dv, dk), F32), pltpu.VMEM((dv, dk), F32)],
        compiler_params=_params("parallel", "parallel", "arbitrary"),
        name="gla_core",
    )(proj, proj, proj, lowrank, proj, proj, proj, lowrank, w2_f, bg_f.reshape(1, -1),
      w2_b, bg_b.reshape(1, -1))


def _gla_gate(of, ob, proj, norm_w, tm=1024):
    m, e = of.shape
    heads = e // GLA_DV
    tm = _tile(m, tm)
    g_col0 = (proj.shape[1] - e) // GLA_DV

    def body(of_ref, ob_ref, g_ref, nw_ref, o_ref):
        o = of_ref[...] + ob_ref[...]
        o = o * lax.rsqrt(jnp.mean(o * o, axis=-1, keepdims=True) + EPS) * nw_ref[...]
        o_ref[...] = (o * _silu(g_ref[...])).astype(BF16)

    tile = pl.BlockSpec((tm, GLA_DV), lambda i, h: (i, h))
    return pl.pallas_call(
        body, grid=(m // tm, heads),
        in_specs=[tile, tile, pl.BlockSpec((tm, GLA_DV), lambda i, h: (i, g_col0 + h)),
                  pl.BlockSpec((1, GLA_DV), lambda i, h: (0, 0))],
        out_specs=tile, out_shape=jax.ShapeDtypeStruct((m, e), BF16),
        compiler_params=_params("parallel", "parallel"), name="gla_gate")(of, ob, proj, norm_w.reshape(1, GLA_DV))


def _gla_layer(x2, h, layer, bsz, seq, gate, w_in, w1_f, w2_f, bg_f, w1_b, w2_b, bg_b, gla_norm_w, w_out):
    e = w_out.shape[1]
    d = w_in.shape[1]
    rank = w1_f.shape[2]
    proj = _matmul("gla_in", h, w_in, layer, 0, w_in.shape[2], [], _no_specs, _identity, F32, seq)
    w1 = jnp.zeros((1, d, LANES), F32).at[0, :, :rank].set(w1_f[layer].astype(F32))
    w1 = w1.at[0, :, rank:2 * rank].set(w1_b[layer].astype(F32))
    lowrank = _matmul("gla_in_gate_rank", h, w1, 0, 0, LANES, [], _no_specs, _identity, F32, seq)
    kw = w2_f.shape[2]
    w2f = jnp.zeros((LANES, kw), BF16).at[:rank].set(w2_f[layer].astype(BF16))
    w2b = jnp.zeros((LANES, kw), BF16).at[rank:2 * rank].set(w2_b[layer].astype(BF16))
    of, ob = _gla_core(proj, lowrank, w2f, bg_f[layer].astype(F32), w2b, bg_b[layer].astype(F32), bsz, seq, e)
    a = _gla_gate(of, ob, proj, gla_norm_w[layer].astype(F32))
    return _out_proj("gla_out", a, w_out, layer, x2, gate, seq)


def _final_norm(x2, w, tm=512):
    m, d = x2.shape
    tm = _tile(m, tm)

    def body(x_ref, w_ref, o_ref):
        xv = x_ref[...]
        o_ref[...] = xv * lax.rsqrt(jnp.mean(xv * xv, axis=-1, keepdims=True) + EPS) * w_ref[...]

    return pl.pallas_call(
        body, grid=(m // tm,),
        in_specs=[pl.BlockSpec((tm, d), lambda i: (i, 0)), pl.BlockSpec((1, d), lambda i: (0, 0))],
        out_specs=pl.BlockSpec((tm, d), lambda i: (i, 0)),
        out_shape=jax.ShapeDtypeStruct((m, d), F32),
        compiler_params=_params("parallel"),
        name="final_norm",
    )(x2, w.reshape(1, d).astype(F32))


def kernel(x, c, mod_w, mod_b, norm_w, na_w_in, na_rpb, na_w_out, s5_w_in, s5_b_re, s5_b_im, s5_c_re, s5_c_im, s5_d, s5_a_re_fwd, s5_a_im_fwd, s5_a_re_bwd, s5_a_im_bwd, s5_log_step_fwd, s5_log_step_bwd, s5_w_glu, s5_b_glu, s5_w_out, gla_w_in, gla_gk_w1_fwd, gla_gk_w2_fwd, gla_gk_b_fwd, gla_gk_w1_bwd, gla_gk_w2_bwd, gla_gk_b_bwd, gla_norm_w, gla_w_out, final_norm_w):
    bsz, seq, d = x.shape
    depth = mod_w.shape[0]
    mod = _modulation(c.astype(F32), mod_w, mod_b.astype(F32))
    x2 = x.astype(F32).reshape(bsz * seq, d)
    for i in range(depth):
        kind, j = i % 3, i // 3
        shift, scale, gate = mod[i, :, :d], mod[i, :, d:2 * d], mod[i, :, 2 * d:]
        h = _norm_modulate(x2, seq, norm_w[i].astype(F32), scale, shift)
        if kind == 0:
            x2 = _na_layer(x2, h, j, bsz, seq, gate, na_w_in, na_rpb, na_w_out)
        elif kind == 1:
            x2 = _s5_layer(x2, h, j, bsz, seq, gate, s5_w_in, s5_b_re, s5_b_im, s5_c_re, s5_c_im, s5_d,
                           s5_a_re_fwd, s5_a_im_fwd, s5_a_re_bwd, s5_a_im_bwd, s5_log_step_fwd,
                           s5_log_step_bwd, s5_w_glu, s5_b_glu, s5_w_out)
        else:
            x2 = _gla_layer(x2, h, j, bsz, seq, gate, gla_w_in, gla_gk_w1_fwd, gla_gk_w2_fwd, gla_gk_b_fwd,
                            gla_gk_w1_bwd, gla_gk_w2_bwd, gla_gk_b_bwd, gla_norm_w, gla_w_out)
    return _final_norm(x2, final_norm_w).reshape(bsz, seq, d)
```

```python
import jax
import jax.numpy as jnp
from jax import lax
from jax.experimental import pallas as pl
from jax.experimental.pallas import tpu as pltpu

EPS = 1e-6
GRID_W = 64
NA_HEAD_DIM = 128
NA_ROWS = 8
NA_COLS = 16
NA_ROW_BLOCK = 8
NA_HEADS_PER_STEP = 2
NA_PIPELINE_GROUP = 4
S5_GROUP = 16
S5_STATE = 64
GLA_DV = 512
GLA_DK = 256
GLA_GATE_NORM = 16.0
GLA_CHUNK = 64
NEG_INF = -1e30

V7X_VMEM_BYTES = 64 * 1024 * 1024
VMEM_LIMIT = V7X_VMEM_BYTES - 8 * 1024 * 1024
LANES = 128
SUBLANES = 8
BF16 = jnp.bfloat16
F32 = jnp.float32


def _params(*sem):
    return pltpu.CompilerParams(dimension_semantics=sem, vmem_limit_bytes=VMEM_LIMIT)


def _silu(x):
    return x * jax.nn.sigmoid(x)


def _log_sigmoid(x):
    return jnp.minimum(x, 0.0) - jnp.log1p(jnp.exp(-jnp.abs(x)))


def _dot(a, b):
    return jnp.dot(a, b, preferred_element_type=F32)


def _dot_nt(a, b):
    return lax.dot_general(a, b, (((1,), (1,)), ((), ())), preferred_element_type=F32)


def _dot_tn(a, b):
    return lax.dot_general(a, b, (((0,), (0,)), ((), ())), preferred_element_type=F32)


def _split3(x):
    p1 = x.astype(BF16)
    r1 = x - p1.astype(F32)
    p2 = r1.astype(BF16)
    p3 = (r1 - p2.astype(F32)).astype(BF16)
    return p1, p2, p3


def _dot_nt_f32(a, b):
    a1, a2, a3 = _split3(a)
    b1, b2, b3 = _split3(b)
    return (_dot_nt(a1, b1) + (_dot_nt(a1, b2) + _dot_nt(a2, b1))
            + (_dot_nt(a1, b3) + _dot_nt(a2, b2) + _dot_nt(a3, b1)))


def _tile(n, want):
    t = min(want, n)
    assert n % t == 0
    return t


def _matmul(name, a, w, layer, col0, n, e_ins, e_specs, epilogue, out_dtype, seq, tm=1024, tn=512):
    m, k = a.shape
    tm = _tile(seq, tm)
    tn = _tile(n, tn)
    assert col0 % tn == 0
    jb = col0 // tn
    n_e = len(e_ins)

    def body(a_ref, w_ref, *rest):
        e_refs, o_ref, wb_ref = rest[:n_e], rest[n_e], rest[n_e + 1]

        @pl.when(pl.program_id(1) == 0)
        def _():
            wb_ref[...] = w_ref[...].astype(BF16)

        o_ref[...] = epilogue(_dot(a_ref[...], wb_ref[...]), *e_refs).astype(out_dtype)

    return pl.pallas_call(
        body,
        grid=(n // tn, m // tm),
        in_specs=[pl.BlockSpec((tm, k), lambda j, i: (i, 0)),
                  pl.BlockSpec((None, k, tn), lambda j, i: (layer, 0, jb + j))] + list(e_specs(tm, tn)),
        out_specs=pl.BlockSpec((tm, tn), lambda j, i: (i, j)),
        out_shape=jax.ShapeDtypeStruct((m, n), out_dtype),
        scratch_shapes=[pltpu.VMEM((k, tn), BF16)],
        compiler_params=_params("parallel", "arbitrary"),
        name=name,
    )(a, w, *e_ins)


def _no_specs(tm, tn):
    return []


def _identity(acc):
    return acc


def _out_proj(name, a, w, layer, x2, gate, seq):
    bsz, d = gate.shape

    def specs(tm, tn):
        tiles_per_seq = seq // tm
        return [pl.BlockSpec((tm, tn), lambda j, i: (i, j)),
                pl.BlockSpec((1, 1, tn), lambda j, i: (i // tiles_per_seq, 0, j))]

    def epilogue(acc, x_ref, gate_ref):
        return x_ref[...] + gate_ref[0] * acc

    return _matmul(name, a, w, layer, 0, d, [x2, gate.reshape(bsz, 1, d)], specs, epilogue, F32, seq)


def _modulation(c, mod_w, mod_b):
    depth, d, n = mod_w.shape
    bsz = c.shape[0]
    c_pad = jnp.zeros((SUBLANES, d), F32).at[:bsz].set(c)
    tn = _tile(n, 512)

    def body(c_ref, w_ref, b_ref, o_ref):
        act = _silu(c_ref[...]).astype(BF16)
        o_ref[0] = _dot(act, w_ref[0].astype(BF16)) + b_ref[0]

    out = pl.pallas_call(
        body,
        grid=(depth, n // tn),
        in_specs=[pl.BlockSpec((SUBLANES, d), lambda i, j: (0, 0)),
                  pl.BlockSpec((1, d, tn), lambda i, j: (i, 0, j)),
                  pl.BlockSpec((1, 1, tn), lambda i, j: (i, 0, j))],
        out_specs=pl.BlockSpec((1, SUBLANES, tn), lambda i, j: (i, 0, j)),
        out_shape=jax.ShapeDtypeStruct((depth, SUBLANES, n), F32),
        compiler_params=_params("parallel", "parallel"),
        name="modulation",
    )(c_pad, mod_w, mod_b.reshape(depth, 1, n))
    return out[:, :bsz]


def _norm_modulate(x2, seq, norm_w, scale, shift, tm=512):
    m, d = x2.shape
    tm = _tile(seq, tm)
    tiles_per_seq = seq // tm
    bsz = m // seq

    def body(x_ref, nw_ref, sc_ref, sh_ref, o_ref):
        xv = x_ref[...]
        y = xv * lax.rsqrt(jnp.mean(xv * xv, axis=-1, keepdims=True) + EPS) * nw_ref[...]
        o_ref[...] = (y * (1.0 + sc_ref[0]) + sh_ref[0]).astype(BF16)

    vec_spec = pl.BlockSpec((1, 1, d), lambda i: (i // tiles_per_seq, 0, 0))
    return pl.pallas_call(
        body, grid=(m // tm,),
        in_specs=[pl.BlockSpec((tm, d), lambda i: (i, 0)), pl.BlockSpec((1, d), lambda i: (0, 0)),
                  vec_spec, vec_spec],
        out_specs=pl.BlockSpec((tm, d), lambda i: (i, 0)),
        out_shape=jax.ShapeDtypeStruct((m, d), BF16),
        compiler_params=_params("parallel"),
        name="norm_modulate",
    )(x2, norm_w.reshape(1, d), scale.reshape(bsz, 1, d), shift.reshape(bsz, 1, d))


def _na_bias_table(rpb):
    w = GRID_W
    qc = jnp.arange(w)
    kc = jnp.arange(w)
    col_start = jnp.clip(qc - NA_COLS // 2, 0, w - NA_COLS)
    col_ok = (kc[None, :] >= col_start[:, None]) & (kc[None, :] < col_start[:, None] + NA_COLS)
    dc = kc[None, :] - qc[:, None] + NA_COLS - 1
    onehot = ((dc[None] == jnp.arange(2 * NA_COLS - 1)[:, None, None]) & col_ok[None]).astype(F32)
    cexp = jnp.einsum('hdc,cqk->hdqk', rpb.astype(F32), onehot, precision=lax.Precision.HIGHEST)
    cexp = jnp.where(col_ok[None, None], cexp, NEG_INF)
    tables = [jnp.concatenate([cexp[:, t + m] for m in range(NA_ROWS)], axis=-1) for t in range(NA_ROWS)]
    return jnp.stack(tables, axis=1)


def _na_attention(qkv, g, bias, bsz, seq):
    e = g.shape[1]
    heads = e // NA_HEAD_DIM
    rows = seq // GRID_W
    tq = NA_ROW_BLOCK * GRID_W
    nblk = rows // NA_ROW_BLOCK
    assert nblk >= 2 and rows % NA_ROW_BLOCK == 0
    half = tq // 2
    hp = NA_HEADS_PER_STEP
    hw = hp * NA_HEAD_DIM
    win = NA_ROWS * GRID_W
    half_rows = NA_ROWS // 2
    scale = NA_HEAD_DIM ** -0.5
    assert heads % hp == 0 and NA_ROW_BLOCK == NA_ROWS

    def body(q_ref, kp_ref, kc_ref, kn_ref, vp_ref, vc_ref, vn_ref, g_ref, b_ref, o_ref,
             kcat_ref, vcat_ref):
        kcat_ref[:half] = kp_ref[...]
        kcat_ref[half:half + tq] = kc_ref[...]
        kcat_ref[half + tq:] = kn_ref[...]
        vcat_ref[:half] = vp_ref[...]
        vcat_ref[half:half + tq] = vc_ref[...]
        vcat_ref[half + tq:] = vn_ref[...]

        def attend(r0):
            def place(qr, h):
                r = r0 + qr
                row_start = min(max(r - half_rows, 0), rows - NA_ROWS)
                start = (row_start - r0 + half_rows) * GRID_W
                return (slice(qr * GRID_W, (qr + 1) * GRID_W), slice(h * NA_HEAD_DIM, (h + 1) * NA_HEAD_DIM),
                        slice(start, start + win), row_start - r + NA_ROWS - 1)

            def scores(qr, h):
                q_rows, lanes, keys, table = place(qr, h)
                return _dot_nt(q_ref[q_rows, lanes], kcat_ref[keys, lanes]) * scale + b_ref[h, table]

            def finish(qr, h, s):
                q_rows, lanes, keys, _ = place(qr, h)
                p = jnp.exp(s - jnp.max(s, axis=-1, keepdims=True))
                denom = jnp.sum(p, axis=-1, keepdims=True)
                o = _dot(p.astype(BF16), vcat_ref[keys, lanes]) / denom
                o_ref[q_rows, lanes] = (o * _silu(g_ref[q_rows, lanes])).astype(BF16)

            work = [(qr, h) for qr in range(NA_ROW_BLOCK) for h in range(hp)]
            groups = [work[i:i + NA_PIPELINE_GROUP] for i in range(0, len(work), NA_PIPELINE_GROUP)]
            pending = [scores(*w) for w in groups[0]]
            for gi, grp in enumerate(groups):
                ahead = [scores(*w) for w in groups[gi + 1]] if gi + 1 < len(groups) else []
                for w, s in zip(grp, pending):
                    finish(*w, s)
                pending = ahead

        blk = pl.program_id(2)
        pl.when(blk == 0)(lambda: attend(0))
        pl.when(blk == nblk - 1)(lambda: attend(rows - NA_ROW_BLOCK))
        pl.when((blk > 0) & (blk < nblk - 1))(lambda: attend(NA_ROW_BLOCK))

    def cur(col0):
        return pl.BlockSpec((tq, hw), lambda b, h, i: (b * nblk + i, col0 + h))

    def above(col0):
        return pl.BlockSpec((half, hw), lambda b, h, i: (jnp.maximum(2 * (b * nblk + i) - 1, 0), col0 + h))

    def below(col0):
        last = 2 * bsz * nblk - 1
        return pl.BlockSpec((half, hw), lambda b, h, i: (jnp.minimum(2 * (b * nblk + i) + 2, last), col0 + h))

    kcol, vcol = heads // hp, 2 * heads // hp
    return pl.pallas_call(
        body,
        grid=(bsz, heads // hp, nblk),
        in_specs=[cur(0),
                  above(kcol), cur(kcol), below(kcol),
                  above(vcol), cur(vcol), below(vcol),
                  cur(0),
                  pl.BlockSpec((hp, NA_ROWS, GRID_W, win), lambda b, h, i: (h, 0, 0, 0))],
        out_specs=cur(0),
        out_shape=jax.ShapeDtypeStruct((bsz * seq, e), BF16),
        scratch_shapes=[pltpu.VMEM((2 * tq, hw), BF16), pltpu.VMEM((2 * tq, hw), BF16)],
        compiler_params=_params("parallel", "parallel", "arbitrary"),
        name="na_attention",
    )(qkv, qkv, qkv, qkv, qkv, qkv, qkv, g, bias)


def _na_layer(x2, h, layer, bsz, seq, gate, w_in, rpb, w_out):
    e = w_out.shape[1]
    qkv = _matmul("na_in_qkv", h, w_in, layer, 0, 3 * e, [], _no_specs, _identity, BF16, seq)
    g = _matmul("na_in_gate", h, w_in, layer, 3 * e, e, [], _no_specs, _identity, F32, seq)
    a = _na_attention(qkv, g, _na_bias_table(rpb[layer]), bsz, seq)
    return _out_proj("na_out", a, w_out, layer, x2, gate, seq)


S5_CHUNK = 64


def _s5_powers(a_re, a_im, log_step, b_re_t, b_im_t, n_pow):
    n_in, rows, lanes = b_re_t.shape

    def body(are_ref, aim_ref, ls_ref, bre_ref, bim_ref, pre_ref, pim_ref, bbre_ref, bbim_ref):
        dt = jnp.exp(ls_ref[...])
        are, aim = are_ref[...], aim_ref[...]
        mag = jnp.exp(are * dt)
        lam_re = mag * jnp.cos(aim * dt)
        lam_im = mag * jnp.sin(aim * dt)
        den = are * are + aim * aim
        nr = lam_re - 1.0
        r_re = (nr * are + lam_im * aim) / den
        r_im = (lam_im * are - nr * aim) / den
        for i in range(n_in):
            bbre_ref[i] = r_re * bre_ref[i] - r_im * bim_ref[i]
            bbim_ref[i] = r_re * bim_ref[i] + r_im * bre_ref[i]

        def power(tau, carry):
            t = jnp.asarray(tau, F32)
            dtv = jnp.exp(ls_ref[...])
            m = jnp.exp(t * (are_ref[...] * dtv))
            ang = t * (aim_ref[...] * dtv)
            pre_ref[tau] = m * jnp.cos(ang)
            pim_ref[tau] = m * jnp.sin(ang)
            return carry

        lax.fori_loop(0, n_pow, power, 0)

    pw = jax.ShapeDtypeStruct((n_pow, rows, lanes), F32)
    bb = jax.ShapeDtypeStruct((n_in, rows, lanes), F32)
    return pl.pallas_call(body, out_shape=(pw, pw, bb, bb), name="s5_powers",
                          compiler_params=pltpu.CompilerParams(vmem_limit_bytes=VMEM_LIMIT),
                          )(a_re, a_im, log_step, b_re_t, b_im_t)


def _s5_tables(lr_in, li_in, xa, xb, lr_out, li_out, ca, cb,
               lhs_b, lhs_f, br2_f, bi2_f, br2_b, bi2_b, ca2, cb2):
    groups, tc, width = lr_in.shape
    n_in = xa.shape[1]

    def body(lri, lii, xa_r, xb_r, lro, lio, ca_r, cb_r, lb, lf, brf, bif, brb, bib, ca2_r, cb2_r,
             pin_ref, pout_ref, dtab_ref):
        for s in range(tc):
            rows = slice(s * n_in, (s + 1) * n_in)
            pin_ref[0, rows, :] = (lri[0, s:s + 1, :] * xa_r[0] + lii[0, s:s + 1, :] * xb_r[0]).astype(BF16)
            pout_ref[0, rows, :] = (lro[0, s:s + 1, :] * ca_r[0] + lio[0, s:s + 1, :] * cb_r[0]).astype(BF16)

        def gcat(br, bi):
            return jnp.concatenate(
                [br[0, j:j + 1, :] * ca2_r[0] + bi[0, j:j + 1, :] * cb2_r[0] for j in range(n_in)], axis=0)

        dtab_ref[0] = _dot_nt_f32(lb[0], gcat(brb, bib)) + _dot_nt_f32(lf[0], gcat(brf, bif))

    def spec(a):
        return pl.BlockSpec((1,) + a.shape[1:], lambda g: (g, 0, 0))

    ins = [lr_in, li_in, xa, xb, lr_out, li_out, ca, cb, lhs_b, lhs_f, br2_f, bi2_f, br2_b, bi2_b, ca2, cb2]
    pshape = jax.ShapeDtypeStruct((groups, tc * n_in, width), BF16)
    dshape = jax.ShapeDtypeStruct((groups, 2 * tc, n_in * n_in), F32)
    return pl.pallas_call(
        body, grid=(groups,), in_specs=[spec(a) for a in ins],
        out_specs=(spec(pshape), spec(pshape), spec(dshape)),
        out_shape=(pshape, pshape, dshape),
        compiler_params=_params("parallel"), name="s5_tables")(*ins)


def _s5_toeplitz(u, drow, p_in, p_out_t, mu, skip, bsz):
    groups, rows, v = u.shape
    n_in = drow.shape[1]
    tc = v // n_in
    per_tile = LANES // n_in
    n_tiles = tc // per_tile
    nch = rows // bsz
    w4 = p_in.shape[2]
    half = w4 // 2

    def body(u_ref, d_ref, pin_ref, pout_ref, mu_ref, skip_ref, z_ref, m_ref, sl_ref, spf_ref, spb_ref):
        taps = d_ref[0]
        for b in range(per_tile):
            off = (per_tile - 1 - b) * n_in
            shifted = taps if off == 0 else pltpu.roll(taps, taps.shape[1] - off, axis=1)
            shifted = shifted.astype(BF16)
            for a in range(n_tiles):
                s = a * per_tile + b
                lo = (n_tiles - 1 - a) * LANES
                m_ref[s * n_in:(s + 1) * n_in, :] = shifted[:, lo:lo + v]
        uv = u_ref[0]
        sl_ref[...] = _dot(uv, pin_ref[0])
        mr, mi = mu_ref[0, 0:1, :], mu_ref[0, 1:2, :]
        sub = lax.broadcasted_iota(jnp.int32, (SUBLANES, half), 0)

        def step(k, carry):
            out = []
            for bi in range(bsz):
                fr, fi, br, bim = carry[bi]
                rows_f = pl.ds(pl.multiple_of(bi * nch + k * SUBLANES, SUBLANES), SUBLANES)
                rows_b = pl.ds(pl.multiple_of(bi * nch + nch - (k + 1) * SUBLANES, SUBLANES), SUBLANES)
                loc_f, loc_b = sl_ref[rows_f, :], sl_ref[rows_b, :]
                tfr = tfi = tbr = tbi = jnp.zeros((SUBLANES, half), F32)
                for i in range(SUBLANES):
                    ib = SUBLANES - 1 - i
                    tfr, tfi = jnp.where(sub == i, fr, tfr), jnp.where(sub == i, fi, tfi)
                    tbr, tbi = jnp.where(sub == ib, br, tbr), jnp.where(sub == ib, bim, tbi)
                    lfr, lfi = loc_f[i:i + 1, :half], loc_f[i:i + 1, half:]
                    lbr, lbi = loc_b[ib:ib + 1, :half], loc_b[ib:ib + 1, half:]
                    fr, fi = mr * fr - mi * fi + lfr, mr * fi + mi * fr + lfi
                    br, bim = mr * br - mi * bim + lbr, mr * bim + mi * br + lbi
                spf_ref[rows_f, :half] = tfr
                spf_ref[rows_f, half:] = tfi
                spb_ref[rows_b, :half] = tbr
                spb_ref[rows_b, half:] = tbi
                out.append((fr, fi, br, bim))
            return tuple(out)

        zero = jnp.zeros((1, half), F32)
        lax.fori_loop(0, nch // SUBLANES, step, tuple((zero, zero, zero, zero) for _ in range(bsz)))
        lane = lax.broadcasted_iota(jnp.int32, (rows, w4), 1)
        fwd_lane = (lane & (half - 1)) < (half // 2)
        carry_in = jnp.where(fwd_lane, spf_ref[...], spb_ref[...]).astype(BF16)
        y = _dot(uv, m_ref[...]) + _dot_nt(carry_in, pout_ref[0])
        z_ref[0] = jax.nn.gelu(y + skip_ref[0] * uv.astype(F32)).astype(BF16)

    def spec(a):
        nd = len(a.shape)
        return pl.BlockSpec((1,) + a.shape[1:], lambda g: (g,) + (0,) * (nd - 1))

    out = jax.ShapeDtypeStruct((groups, rows, v), BF16)
    return pl.pallas_call(
        body, grid=(groups,),
        in_specs=[spec(u), spec(drow), spec(p_in), spec(p_out_t), spec(mu), spec(skip)],
        out_specs=spec(out), out_shape=out,
        scratch_shapes=[pltpu.VMEM((v, v), BF16), pltpu.VMEM((rows, w4), F32),
                        pltpu.VMEM((rows, w4), F32), pltpu.VMEM((rows, w4), F32)],
        compiler_params=_params("parallel"), name="s5_toeplitz")(u, drow, p_in, p_out_t, mu, skip)


def _s5_mix(u2, bsz, seq, e, b_re, b_im, c_re, c_im, d_skip, a_re_f, a_im_f, a_re_b, a_im_b, ls_f, ls_b):
    groups, states, n_in = b_re.shape
    tc = S5_CHUNK
    nch = seq // tc
    flat = lambda t: t.astype(F32).reshape(groups * states // LANES, LANES)
    bt = lambda t: t.astype(F32).transpose(2, 0, 1).reshape(n_in, groups * states // LANES, LANES)
    cat = lambda *ts: jnp.concatenate(ts, axis=-1)
    flip = lambda t: jnp.flip(t, axis=1)

    def direction(a_re, a_im, ls):
        pr, pi, bbr, bbi = _s5_powers(flat(a_re), flat(a_im), flat(ls), bt(b_re), bt(b_im), tc + 1)
        per_group = lambda t: t.reshape(t.shape[0], groups, states).transpose(1, 0, 2)
        return per_group(pr), per_group(pi), per_group(bbr), per_group(bbi)

    pfr, pfi, bfr, bfi = direction(a_re_f, a_im_f, ls_f)
    pbr, pbi, bbr, bbi = direction(a_re_b, a_im_b, ls_b)
    cr, ci = c_re.astype(F32), c_im.astype(F32)

    lr_in = cat(flip(pfr[:, :tc]), pbr[:, :tc], flip(pfr[:, :tc]), pbr[:, :tc])
    li_in = cat(flip(pfi[:, :tc]), pbi[:, :tc], flip(pfi[:, :tc]), pbi[:, :tc])
    lr_out = cat(pfr[:, 1:], flip(pbr[:, 1:]), pfr[:, 1:], flip(pbr[:, 1:]))
    li_out = cat(pfi[:, 1:], flip(pbi[:, 1:]), pfi[:, 1:], flip(pbi[:, 1:]))
    xa, xb = cat(bfr, bbr, bfi, bbi), cat(-bfi, -bbi, bfr, bbr)
    ca, cb = cat(cr, cr, -ci, -ci), cat(-ci, -ci, -cr, -cr)
    lhs_b = jnp.pad(cat(flip(pbr[:, :tc]), flip(pbi[:, :tc])), ((0, 0), (0, tc), (0, 0)))
    lhs_f = jnp.pad(cat(pfr[:, :tc], pfi[:, :tc]), ((0, 0), (tc - 1, 1), (0, 0)))
    ca2, cb2 = cat(cr, -ci), cat(-ci, -cr)
    p_in, p_out_t, dtab = _s5_tables(lr_in, li_in, xa, xb, lr_out, li_out, ca, cb, lhs_b, lhs_f,
                                     cat(bfr, bfr), cat(bfi, bfi), cat(bbr, bbr), cat(bbi, bbi), ca2, cb2)
    drow = dtab.reshape(groups, 2 * tc, n_in, n_in).transpose(0, 2, 1, 3).reshape(groups, n_in, 2 * tc * n_in)
    mu = jnp.stack([cat(pfr[:, tc], pbr[:, tc]), cat(pfi[:, tc], pbi[:, tc])], axis=1)

    skip = jnp.tile(d_skip.astype(F32).reshape(groups, 1, n_in), (1, 1, tc))
    u = u2.reshape(bsz, nch, tc, groups, n_in).transpose(3, 0, 1, 2, 4)
    u = u.reshape(groups, bsz * nch, tc * n_in)
    z = _s5_toeplitz(u, drow, p_in, p_out_t, mu, skip, bsz)
    return z.reshape(groups, bsz, nch, tc, n_in).transpose(1, 2, 3, 0, 4).reshape(bsz * seq, e)


def _s5_glu(z, g, w_glu, layer, b_glu, seq):
    e = z.shape[1]

    def specs(tm, tn):
        tile = pl.BlockSpec((tm, tn), lambda j, i: (i, j))
        return [tile, tile, pl.BlockSpec((1, tn), lambda j, i: (0, j))]

    def epilogue(acc, z_ref, g_ref, b_ref):
        return z_ref[...].astype(F32) * jax.nn.sigmoid(acc + b_ref[...]) * _silu(g_ref[...])

    return _matmul("s5_glu", z, w_glu, layer, 0, e, [z, g, b_glu.reshape(1, e)], specs, epilogue, BF16, seq)


def _s5_layer(x2, h, layer, bsz, seq, gate, w_in, b_re, b_im, c_re, c_im, d_skip,
              a_re_f, a_im_f, a_re_b, a_im_b, log_step_f, log_step_b, w_glu, b_glu, w_out):
    e = w_out.shape[1]
    u2 = _matmul("s5_in_u", h, w_in, layer, 0, e, [], _no_specs, _identity, BF16, seq)
    g = _matmul("s5_in_gate", h, w_in, layer, e, e, [], _no_specs, _identity, F32, seq)
    pick = lambda t: t[layer]
    z = _s5_mix(u2, bsz, seq, e, pick(b_re), pick(b_im), pick(c_re), pick(c_im), pick(d_skip), pick(a_re_f),
                pick(a_im_f), pick(a_re_b), pick(a_im_b), pick(log_step_f), pick(log_step_b))
    a = _s5_glu(z, g, w_glu, layer, b_glu[layer].astype(F32), seq)
    return _out_proj("s5_out", a, w_out, layer, x2, gate, seq)


def _gla_core(proj, lowrank, w2_f, bg_f, w2_b, bg_b, bsz, seq, e, chunks_per_step=4):
    heads = e // GLA_DV
    dk, dv, ch = GLA_DK, GLA_DV, GLA_CHUNK
    rt = min(chunks_per_step * ch, seq)
    ncs = rt // ch
    nb = seq // rt
    q_scale = dk ** -0.5

    def prepare(forward, q_ref, k_ref, v_ref, lr_ref, w2_ref, bg_ref):
        row = lax.broadcasted_iota(jnp.int32, (rt, rt), 0)
        col = lax.broadcasted_iota(jnp.int32, (rt, rt), 1)
        shift = ch.bit_length() - 1
        same = (row >> shift) == (col >> shift)
        if forward:
            cum_mask = same & (col <= row)
            score_mask = cum_mask
        else:
            cum_mask = same & (col >= row)
            score_mask = same & (col > row)
        gk = _log_sigmoid(_dot(lr_ref[...].astype(BF16), w2_ref[...]) + bg_ref[...]) / GLA_GATE_NORM
        tri = jnp.where(cum_mask, 1.0, 0.0).astype(BF16)
        g1, g2, g3 = _split3(gk)
        bcum = _dot(tri, g1) + _dot(tri, g2) + _dot(tri, g3)
        k = k_ref[...]
        v = v_ref[...].astype(BF16)
        q_s = (q_ref[...] * q_scale * jnp.exp(bcum)).astype(BF16)
        k_s = (k * jnp.exp(-bcum)).astype(BF16)
        scores = jnp.where(score_mask, _dot_nt(q_s, k_s), 0.0)
        o_intra = _dot(scores.astype(BF16), v)
        return q_s, k, v, bcum, o_intra

    def advance(forward, c, prep, o_ref, st_ref):
        q_s, k, v, bcum, o_intra = prep
        rows = slice(c * ch, (c + 1) * ch)
        last = c * ch + ch - 1 if forward else c * ch
        b_last = bcum[last:last + 1, :]
        state = st_ref[...]
        o_ref[rows, :] = o_intra[rows] + _dot_nt(q_s[rows], state.astype(BF16))
        k_end = (k[rows] * jnp.exp(b_last - bcum[rows])).astype(BF16)
        st_ref[...] = state * jnp.exp(b_last) + _dot_tn(v[rows], k_end)

    def body(qf, kf, vf, lf, qb, kb, vb, lb, w2f, bgf, w2b, bgb, of, ob, stf, stb):
        @pl.when(pl.program_id(2) == 0)
        def _():
            stf[...] = jnp.zeros_like(stf)
            stb[...] = jnp.zeros_like(stb)

        prep_f = prepare(True, qf, kf, vf, lf, w2f, bgf)
        prep_b = prepare(False, qb, kb, vb, lb, w2b, bgb)
        for c in range(ncs):
            advance(True, c, prep_f, of, stf)
            advance(False, ncs - 1 - c, prep_b, ob, stb)

    def tok(forward):
        return (lambda b, h, n: b * nb + n) if forward else (lambda b, h, n: b * nb + nb - 1 - n)

    def specs(forward):
        t = tok(forward)
        return [pl.BlockSpec((rt, dk), lambda b, h, n: (t(b, h, n), h)),
                pl.BlockSpec((rt, dk), lambda b, h, n: (t(b, h, n), heads + h)),
                pl.BlockSpec((rt, dv), lambda b, h, n: (t(b, h, n), heads + h)),
                pl.BlockSpec((rt, LANES), lambda b, h, n: (t(b, h, n), 0))]

    w2_spec = pl.BlockSpec((LANES, dk), lambda b, h, n: (0, h))
    bg_spec = pl.BlockSpec((1, dk), lambda b, h, n: (0, h))
    out = jax.ShapeDtypeStruct((bsz * seq, e), F32)
    tf, tbk = tok(True), tok(False)
    return pl.pallas_call(
        body,
        grid=(bsz, heads, nb),
        in_specs=specs(True) + specs(False) + [w2_spec, bg_spec, w2_spec, bg_spec],
        out_specs=(pl.BlockSpec((rt, dv), lambda b, h, n: (tf(b, h, n), h)),
                   pl.BlockSpec((rt, dv), lambda b, h, n: (tbk(b, h, n), h))),
        out_shape=(out, out),
        scratch_shapes=[pltpu.VMEM((dv, dk), F32), pltpu.VMEM((dv, dk), F32)],
        compiler_params=_params("parallel", "parallel", "arbitrary"),
        name="gla_core",
    )(proj, proj, proj, lowrank, proj, proj, proj, lowrank, w2_f, bg_f.reshape(1, -1),
      w2_b, bg_b.reshape(1, -1))


def _gla_gate(of, ob, proj, norm_w, tm=1024):
    m, e = of.shape
    heads = e // GLA_DV
    tm = _tile(m, tm)
    g_col0 = (proj.shape[1] - e) // GLA_DV

    def body(of_ref, ob_ref, g_ref, nw_ref, o_ref):
        o = of_ref[...] + ob_ref[...]
        o = o * lax.rsqrt(jnp.mean(o * o, axis=-1, keepdims=True) + EPS) * nw_ref[...]
        o_ref[...] = (o * _silu(g_ref[...])).astype(BF16)

    tile = pl.BlockSpec((tm, GLA_DV), lambda i, h: (i, h))
    return pl.pallas_call(
        body, grid=(m // tm, heads),
        in_specs=[tile, tile, pl.BlockSpec((tm, GLA_DV), lambda i, h: (i, g_col0 + h)),
                  pl.BlockSpec((1, GLA_DV), lambda i, h: (0, 0))],
        out_specs=tile, out_shape=jax.ShapeDtypeStruct((m, e), BF16),
        compiler_params=_params("parallel", "parallel"), name="gla_gate")(of, ob, proj, norm_w.reshape(1, GLA_DV))


def _gla_layer(x2, h, layer, bsz, seq, gate, w_in, w1_f, w2_f, bg_f, w1_b, w2_b, bg_b, gla_norm_w, w_out):
    e = w_out.shape[1]
    d = w_in.shape[1]
    rank = w1_f.shape[2]
    proj = _matmul("gla_in", h, w_in, layer, 0, w_in.shape[2], [], _no_specs, _identity, F32, seq)
    w1 = jnp.zeros((1, d, LANES), F32).at[0, :, :rank].set(w1_f[layer].astype(F32))
    w1 = w1.at[0, :, rank:2 * rank].set(w1_b[layer].astype(F32))
    lowrank = _matmul("gla_in_gate_rank", h, w1, 0, 0, LANES, [], _no_specs, _identity, F32, seq)
    kw = w2_f.shape[2]
    w2f = jnp.zeros((LANES, kw), BF16).at[:rank].set(w2_f[layer].astype(BF16))
    w2b = jnp.zeros((LANES, kw), BF16).at[rank:2 * rank].set(w2_b[layer].astype(BF16))
    of, ob = _gla_core(proj, lowrank, w2f, bg_f[layer].astype(F32), w2b, bg_b[layer].astype(F32), bsz, seq, e)
    a = _gla_gate(of, ob, proj, gla_norm_w[layer].astype(F32))
    return _out_proj("gla_out", a, w_out, layer, x2, gate, seq)


def _final_norm(x2, w, tm=512):
    m, d = x2.shape
    tm = _tile(m, tm)

    def body(x_ref, w_ref, o_ref):
        xv = x_ref[...]
        o_ref[...] = xv * lax.rsqrt(jnp.mean(xv * xv, axis=-1, keepdims=True) + EPS) * w_ref[...]

    return pl.pallas_call(
        body, grid=(m // tm,),
        in_specs=[pl.BlockSpec((tm, d), lambda i: (i, 0)), pl.BlockSpec((1, d), lambda i: (0, 0))],
        out_specs=pl.BlockSpec((tm, d), lambda i: (i, 0)),
        out_shape=jax.ShapeDtypeStruct((m, d), F32),
        compiler_params=_params("parallel"),
        name="final_norm",
    )(x2, w.reshape(1, d).astype(F32))


def kernel(x, c, mod_w, mod_b, norm_w, na_w_in, na_rpb, na_w_out, s5_w_in, s5_b_re, s5_b_im, s5_c_re, s5_c_im, s5_d, s5_a_re_fwd, s5_a_im_fwd, s5_a_re_bwd, s5_a_im_bwd, s5_log_step_fwd, s5_log_step_bwd, s5_w_glu, s5_b_glu, s5_w_out, gla_w_in, gla_gk_w1_fwd, gla_gk_w2_fwd, gla_gk_b_fwd, gla_gk_w1_bwd, gla_gk_w2_bwd, gla_gk_b_bwd, gla_norm_w, gla_w_out, final_norm_w):
    bsz, seq, d = x.shape
    depth = mod_w.shape[0]
    mod = _modulation(c.astype(F32), mod_w, mod_b.astype(F32))
    x2 = x.astype(F32).reshape(bsz * seq, d)
    for i in range(depth):
        kind, j = i % 3, i // 3
        shift, scale, gate = mod[i, :, :d], mod[i, :, d:2 * d], mod[i, :, 2 * d:]
        h = _norm_modulate(x2, seq, norm_w[i].astype(F32), scale, shift)
        if kind == 0:
            x2 = _na_layer(x2, h, j, bsz, seq, gate, na_w_in, na_rpb, na_w_out)
        elif kind == 1:
            x2 = _s5_layer(x2, h, j, bsz, seq, gate, s5_w_in, s5_b_re, s5_b_im, s5_c_re, s5_c_im, s5_d,
                           s5_a_re_fwd, s5_a_im_fwd, s5_a_re_bwd, s5_a_im_bwd, s5_log_step_fwd,
                           s5_log_step_bwd, s5_w_glu, s5_b_glu, s5_w_out)
        else:
            x2 = _gla_layer(x2, h, j, bsz, seq, gate, gla_w_in, gla_gk_w1_fwd, gla_gk_w2_fwd, gla_gk_b_fwd,
                            gla_gk_w1_bwd, gla_gk_w2_bwd, gla_gk_b_bwd, gla_norm_w, gla_w_out)
    return _final_norm(x2, final_norm_w).reshape(bsz, seq, d)
```

```python
import jax
import jax.numpy as jnp
from jax import lax
from jax.experimental import pallas as pl
from jax.experimental.pallas import tpu as pltpu

EPS = 1e-6
GRID_W = 64
NA_HEAD_DIM = 128
NA_ROWS = 8
NA_COLS = 16
NA_ROW_BLOCK = 8
NA_HEADS_PER_STEP = 2
NA_PIPELINE_GROUP = 4
S5_GROUP = 16
S5_STATE = 64
GLA_DV = 512
GLA_DK = 256
GLA_GATE_NORM = 16.0
GLA_CHUNK = 64
NEG_INF = -1e30

V7X_VMEM_BYTES = 64 * 1024 * 1024
VMEM_LIMIT = V7X_VMEM_BYTES - 8 * 1024 * 1024
LANES = 128
SUBLANES = 8
BF16 = jnp.bfloat16
F32 = jnp.float32


def _params(*sem):
    return pltpu.CompilerParams(dimension_semantics=sem, vmem_limit_bytes=VMEM_LIMIT)


def _silu(x):
    return x * jax.nn.sigmoid(x)


def _log_sigmoid(x):
    return jnp.minimum(x, 0.0) - jnp.log1p(jnp.exp(-jnp.abs(x)))


def _dot(a, b):
    return jnp.dot(a, b, preferred_element_type=F32)


def _dot_nt(a, b):
    return lax.dot_general(a, b, (((1,), (1,)), ((), ())), preferred_element_type=F32)


def _dot_tn(a, b):
    return lax.dot_general(a, b, (((0,), (0,)), ((), ())), preferred_element_type=F32)


def _split3(x):
    p1 = x.astype(BF16)
    r1 = x - p1.astype(F32)
    p2 = r1.astype(BF16)
    p3 = (r1 - p2.astype(F32)).astype(BF16)
    return p1, p2, p3


def _dot_nt_f32(a, b):
    a1, a2, a3 = _split3(a)
    b1, b2, b3 = _split3(b)
    return (_dot_nt(a1, b1) + (_dot_nt(a1, b2) + _dot_nt(a2, b1))
            + (_dot_nt(a1, b3) + _dot_nt(a2, b2) + _dot_nt(a3, b1)))


def _tile(n, want):
    t = min(want, n)
    assert n % t == 0
    return t


def _matmul(name, a, w, layer, col0, n, e_ins, e_specs, epilogue, out_dtype, seq, tm=1024, tn=512):
    m, k = a.shape
    tm = _tile(seq, tm)
    tn = _tile(n, tn)
    assert col0 % tn == 0
    jb = col0 // tn
    n_e = len(e_ins)

    def body(a_ref, w_ref, *rest):
        e_refs, o_ref, wb_ref = rest[:n_e], rest[n_e], rest[n_e + 1]

        @pl.when(pl.program_id(1) == 0)
        def _():
            wb_ref[...] = w_ref[...].astype(BF16)

        o_ref[...] = epilogue(_dot(a_ref[...], wb_ref[...]), *e_refs).astype(out_dtype)

    return pl.pallas_call(
        body,
        grid=(n // tn, m // tm),
        in_specs=[pl.BlockSpec((tm, k), lambda j, i: (i, 0)),
                  pl.BlockSpec((None, k, tn), lambda j, i: (layer, 0, jb + j))] + list(e_specs(tm, tn)),
        out_specs=pl.BlockSpec((tm, tn), lambda j, i: (i, j)),
        out_shape=jax.ShapeDtypeStruct((m, n), out_dtype),
        scratch_shapes=[pltpu.VMEM((k, tn), BF16)],
        compiler_params=_params("parallel", "arbitrary"),
        name=name,
    )(a, w, *e_ins)


def _no_specs(tm, tn):
    return []


def _identity(acc):
    return acc


def _out_proj(name, a, w, layer, x2, gate, seq):
    bsz, d = gate.shape

    def specs(tm, tn):
        tiles_per_seq = seq // tm
        return [pl.BlockSpec((tm, tn), lambda j, i: (i, j)),
                pl.BlockSpec((1, 1, tn), lambda j, i: (i // tiles_per_seq, 0, j))]

    def epilogue(acc, x_ref, gate_ref):
        return x_ref[...] + gate_ref[0] * acc

    return _matmul(name, a, w, layer, 0, d, [x2, gate.reshape(bsz, 1, d)], specs, epilogue, F32, seq)


def _modulation(c, mod_w, mod_b):
    depth, d, n = mod_w.shape
    bsz = c.shape[0]
    c_pad = jnp.zeros((SUBLANES, d), F32).at[:bsz].set(c)
    tn = _tile(n, 512)

    def body(c_ref, w_ref, b_ref, o_ref):
        act = _silu(c_ref[...]).astype(BF16)
        o_ref[0] = _dot(act, w_ref[0].astype(BF16)) + b_ref[0]

    out = pl.pallas_call(
        body,
        grid=(depth, n // tn),
        in_specs=[pl.BlockSpec((SUBLANES, d), lambda i, j: (0, 0)),
                  pl.BlockSpec((1, d, tn), lambda i, j: (i, 0, j)),
                  pl.BlockSpec((1, 1, tn), lambda i, j: (i, 0, j))],
        out_specs=pl.BlockSpec((1, SUBLANES, tn), lambda i, j: (i, 0, j)),
        out_shape=jax.ShapeDtypeStruct((depth, SUBLANES, n), F32),
        compiler_params=_params("parallel", "parallel"),
        name="modulation",
    )(c_pad, mod_w, mod_b.reshape(depth, 1, n))
    return out[:, :bsz]


def _norm_modulate(x2, seq, norm_w, scale, shift, tm=512):
    m, d = x2.shape
    tm = _tile(seq, tm)
    tiles_per_seq = seq // tm
    bsz = m // seq

    def body(x_ref, nw_ref, sc_ref, sh_ref, o_ref):
        xv = x_ref[...]
        y = xv * lax.rsqrt(jnp.mean(xv * xv, axis=-1, keepdims=True) + EPS) * nw_ref[...]
        o_ref[...] = (y * (1.0 + sc_ref[0]) + sh_ref[0]).astype(BF16)

    vec_spec = pl.BlockSpec((1, 1, d), lambda i: (i // tiles_per_seq, 0, 0))
    return pl.pallas_call(
        body, grid=(m // tm,),
        in_specs=[pl.BlockSpec((tm, d), lambda i: (i, 0)), pl.BlockSpec((1, d), lambda i: (0, 0)),
                  vec_spec, vec_spec],
        out_specs=pl.BlockSpec((tm, d), lambda i: (i, 0)),
        out_shape=jax.ShapeDtypeStruct((m, d), BF16),
        compiler_params=_params("parallel"),
        name="norm_modulate",
    )(x2, norm_w.reshape(1, d), scale.reshape(bsz, 1, d), shift.reshape(bsz, 1, d))


def _na_bias_table(rpb):
    w = GRID_W
    qc = jnp.arange(w)
    kc = jnp.arange(w)
    col_start = jnp.clip(qc - NA_COLS // 2, 0, w - NA_COLS)
    col_ok = (kc[None, :] >= col_start[:, None]) & (kc[None, :] < col_start[:, None] + NA_COLS)
    dc = kc[None, :] - qc[:, None] + NA_COLS - 1
    onehot = ((dc[None] == jnp.arange(2 * NA_COLS - 1)[:, None, None]) & col_ok[None]).astype(F32)
    cexp = jnp.einsum('hdc,cqk->hdqk', rpb.astype(F32), onehot, precision=lax.Precision.HIGHEST)
    cexp = jnp.where(col_ok[None, None], cexp, NEG_INF)
    tables = [jnp.concatenate([cexp[:, t + m] for m in range(NA_ROWS)], axis=-1) for t in range(NA_ROWS)]
    return jnp.stack(tables, axis=1)


def _na_attention(qkv, g, bias, bsz, seq):
    e = g.shape[1]
    heads = e // NA_HEAD_DIM
    rows = seq // GRID_W
    tq = NA_ROW_BLOCK * GRID_W
    nblk = rows // NA_ROW_BLOCK
    assert nblk >= 2 and rows % NA_ROW_BLOCK == 0
    half = tq // 2
    hp = NA_HEADS_PER_STEP
    hw = hp * NA_HEAD_DIM
    win = NA_ROWS * GRID_W
    half_rows = NA_ROWS // 2
    scale = NA_HEAD_DIM ** -0.5
    assert heads % hp == 0 and NA_ROW_BLOCK == NA_ROWS

    def body(q_ref, kp_ref, kc_ref, kn_ref, vp_ref, vc_ref, vn_ref, g_ref, b_ref, o_ref,
             kcat_ref, vcat_ref):
        kcat_ref[:half] = kp_ref[...]
        kcat_ref[half:half + tq] = kc_ref[...]
        kcat_ref[half + tq:] = kn_ref[...]
        vcat_ref[:half] = vp_ref[...]
        vcat_ref[half:half + tq] = vc_ref[...]
        vcat_ref[half + tq:] = vn_ref[...]

        def attend(r0):
            def place(qr, h):
                r = r0 + qr
                row_start = min(max(r - half_rows, 0), rows - NA_ROWS)
                start = (row_start - r0 + half_rows) * GRID_W
                return (slice(qr * GRID_W, (qr + 1) * GRID_W), slice(h * NA_HEAD_DIM, (h + 1) * NA_HEAD_DIM),
                        slice(start, start + win), row_start - r + NA_ROWS - 1)

            def scores(qr, h):
                q_rows, lanes, keys, table = place(qr, h)
                return _dot_nt(q_ref[q_rows, lanes], kcat_ref[keys, lanes]) * scale + b_ref[h, table]

            def finish(qr, h, s):
                q_rows, lanes, keys, _ = place(qr, h)
                p = jnp.exp(s - jnp.max(s, axis=-1, keepdims=True))
                denom = jnp.sum(p, axis=-1, keepdims=True)
                o = _dot(p.astype(BF16), vcat_ref[keys, lanes]) / denom
                o_ref[q_rows, lanes] = (o * _silu(g_ref[q_rows, lanes])).astype(BF16)

            work = [(qr, h) for qr in range(NA_ROW_BLOCK) for h in range(hp)]
            groups = [work[i:i + NA_PIPELINE_GROUP] for i in range(0, len(work), NA_PIPELINE_GROUP)]
            pending = [scores(*w) for w in groups[0]]
            for gi, grp in enumerate(groups):
                ahead = [scores(*w) for w in groups[gi + 1]] if gi + 1 < len(groups) else []
                for w, s in zip(grp, pending):
                    finish(*w, s)
                pending = ahead

        blk = pl.program_id(2)
        pl.when(blk == 0)(lambda: attend(0))
        pl.when(blk == nblk - 1)(lambda: attend(rows - NA_ROW_BLOCK))
        pl.when((blk > 0) & (blk < nblk - 1))(lambda: attend(NA_ROW_BLOCK))

    def cur(col0):
        return pl.BlockSpec((tq, hw), lambda b, h, i: (b * nblk + i, col0 + h))

    def above(col0):
        return pl.BlockSpec((half, hw), lambda b, h, i: (jnp.maximum(2 * (b * nblk + i) - 1, 0), col0 + h))

    def below(col0):
        last = 2 * bsz * nblk - 1
        return pl.BlockSpec((half, hw), lambda b, h, i: (jnp.minimum(2 * (b * nblk + i) + 2, last), col0 + h))

    kcol, vcol = heads // hp, 2 * heads // hp
    return pl.pallas_call(
        body,
        grid=(bsz, heads // hp, nblk),
        in_specs=[cur(0),
                  above(kcol), cur(kcol), below(kcol),
                  above(vcol), cur(vcol), below(vcol),
                  cur(0),
                  pl.BlockSpec((hp, NA_ROWS, GRID_W, win), lambda b, h, i: (h, 0, 0, 0))],
        out_specs=cur(0),
        out_shape=jax.ShapeDtypeStruct((bsz * seq, e), BF16),
        scratch_shapes=[pltpu.VMEM((2 * tq, hw), BF16), pltpu.VMEM((2 * tq, hw), BF16)],
        compiler_params=_params("parallel", "parallel", "arbitrary"),
        name="na_attention",
    )(qkv, qkv, qkv, qkv, qkv, qkv, qkv, g, bias)


def _na_layer(x2, h, layer, bsz, seq, gate, w_in, rpb, w_out):
    e = w_out.shape[1]
    qkv = _matmul("na_in_qkv", h, w_in, layer, 0, 3 * e, [], _no_specs, _identity, BF16, seq)
    g = _matmul("na_in_gate", h, w_in, layer, 3 * e, e, [], _no_specs, _identity, F32, seq)
    a = _na_attention(qkv, g, _na_bias_table(rpb[layer]), bsz, seq)
    return _out_proj("na_out", a, w_out, layer, x2, gate, seq)


S5_CHUNK = 64


def _s5_powers(a_re, a_im, log_step, b_re_t, b_im_t, n_pow):
    n_in, rows, lanes = b_re_t.shape

    def body(are_ref, aim_ref, ls_ref, bre_ref, bim_ref, pre_ref, pim_ref, bbre_ref, bbim_ref):
        dt = jnp.exp(ls_ref[...])
        are, aim = are_ref[...], aim_ref[...]
        mag = jnp.exp(are * dt)
        lam_re = mag * jnp.cos(aim * dt)
        lam_im = mag * jnp.sin(aim * dt)
        den = are * are + aim * aim
        nr = lam_re - 1.0
        r_re = (nr * are + lam_im * aim) / den
        r_im = (lam_im * are - nr * aim) / den
        for i in range(n_in):
            bbre_ref[i] = r_re * bre_ref[i] - r_im * bim_ref[i]
            bbim_ref[i] = r_re * bim_ref[i] + r_im * bre_ref[i]

        def power(tau, carry):
            t = jnp.asarray(tau, F32)
            dtv = jnp.exp(ls_ref[...])
            m = jnp.exp(t * (are_ref[...] * dtv))
            ang = t * (aim_ref[...] * dtv)
            pre_ref[tau] = m * jnp.cos(ang)
            pim_ref[tau] = m * jnp.sin(ang)
            return carry

        lax.fori_loop(0, n_pow, power, 0)

    pw = jax.ShapeDtypeStruct((n_pow, rows, lanes), F32)
    bb = jax.ShapeDtypeStruct((n_in, rows, lanes), F32)
    return pl.pallas_call(body, out_shape=(pw, pw, bb, bb), name="s5_powers",
                          compiler_params=pltpu.CompilerParams(vmem_limit_bytes=VMEM_LIMIT),
                          )(a_re, a_im, log_step, b_re_t, b_im_t)


def _s5_tables(lr_in, li_in, xa, xb, lr_out, li_out, ca, cb,
               lhs_b, lhs_f, br2_f, bi2_f, br2_b, bi2_b, ca2, cb2):
    groups, tc, width = lr_in.shape
    n_in = xa.shape[1]

    def body(lri, lii, xa_r, xb_r, lro, lio, ca_r, cb_r, lb, lf, brf, bif, brb, bib, ca2_r, cb2_r,
             pin_ref, pout_ref, dtab_ref):
        for s in range(tc):
            rows = slice(s * n_in, (s + 1) * n_in)
            pin_ref[0, rows, :] = (lri[0, s:s + 1, :] * xa_r[0] + lii[0, s:s + 1, :] * xb_r[0]).astype(BF16)
            pout_ref[0, rows, :] = (lro[0, s:s + 1, :] * ca_r[0] + lio[0, s:s + 1, :] * cb_r[0]).astype(BF16)

        def gcat(br, bi):
            return jnp.concatenate(
                [br[0, j:j + 1, :] * ca2_r[0] + bi[0, j:j + 1, :] * cb2_r[0] for j in range(n_in)], axis=0)

        dtab_ref[0] = _dot_nt_f32(lb[0], gcat(brb, bib)) + _dot_nt_f32(lf[0], gcat(brf, bif))

    def spec(a):
        return pl.BlockSpec((1,) + a.shape[1:], lambda g: (g, 0, 0))

    ins = [lr_in, li_in, xa, xb, lr_out, li_out, ca, cb, lhs_b, lhs_f, br2_f, bi2_f, br2_b, bi2_b, ca2, cb2]
    pshape = jax.ShapeDtypeStruct((groups, tc * n_in, width), BF16)
    dshape = jax.ShapeDtypeStruct((groups, 2 * tc, n_in * n_in), F32)
    return pl.pallas_call(
        body, grid=(groups,), in_specs=[spec(a) for a in ins],
        out_specs=(spec(pshape), spec(pshape), spec(dshape)),
        out_shape=(pshape, pshape, dshape),
        compiler_params=_params("parallel"), name="s5_tables")(*ins)


def _s5_pack(u2, tc, rb=128):
    m, e = u2.shape
    r_total = m // tc
    rb = _tile(r_total, rb)
    gpl = LANES // S5_GROUP

    def body(x_ref, o_ref, xf_ref):
        xf_ref[...] = x_ref[...].astype(F32)
        for s in range(tc):
            slab_t = xf_ref[:, s, :].T
            for gl in range(gpl):
                o_ref[gl, s * S5_GROUP:(s + 1) * S5_GROUP, :] = (
                    slab_t[gl * S5_GROUP:(gl + 1) * S5_GROUP, :].astype(BF16))

    return pl.pallas_call(
        body, grid=(r_total // rb, e // LANES),
        in_specs=[pl.BlockSpec((rb, tc, LANES), lambda i, j: (i, 0, j))],
        out_specs=pl.BlockSpec((gpl, tc * S5_GROUP, rb), lambda i, j: (j, 0, i)),
        out_shape=jax.ShapeDtypeStruct((e // S5_GROUP, tc * S5_GROUP, r_total), BF16),
        scratch_shapes=[pltpu.VMEM((rb, tc, LANES), F32)],
        compiler_params=_params("parallel", "parallel"), name="s5_pack")(u2.reshape(r_total, tc, e))


def _s5_unpack(zt, tc, rb=128):
    groups, v, r_total = zt.shape
    e = groups * S5_GROUP
    rb = _tile(r_total, rb)
    gpl = LANES // S5_GROUP

    def body(z_ref, o_ref, of_ref):
        for t in range(tc):
            rows = slice(t * S5_GROUP, (t + 1) * S5_GROUP)
            slab_t = jnp.concatenate([z_ref[gl, rows, :].astype(F32) for gl in range(gpl)], axis=0)
            of_ref[:, t, :] = slab_t.T
        o_ref[...] = of_ref[...].astype(BF16)

    out = pl.pallas_call(
        body, grid=(r_total // rb, e // LANES),
        in_specs=[pl.BlockSpec((gpl, v, rb), lambda i, j: (j, 0, i))],
        out_specs=pl.BlockSpec((rb, tc, LANES), lambda i, j: (i, 0, j)),
        out_shape=jax.ShapeDtypeStruct((r_total, tc, e), BF16),
        scratch_shapes=[pltpu.VMEM((rb, tc, LANES), F32)],
        compiler_params=_params("parallel", "parallel"), name="s5_unpack")(zt)
    return out.reshape(r_total * tc, e)


def _s5_toeplitz(ut, drow, p_in, p_out_t, mu, skip, bsz):
    groups, v, rows = ut.shape
    n_in = drow.shape[1]
    tc = v // n_in
    per_tile = LANES // n_in
    n_tiles = tc // per_tile
    nch = rows // bsz
    w4 = p_in.shape[2]
    half = w4 // 2

    def body(u_ref, d_ref, pin_ref, pout_ref, mu_ref, skip_ref, z_ref, m_ref, sl_ref, spf_ref, spb_ref):
        taps = d_ref[0]
        for b in range(per_tile):
            off = (per_tile - 1 - b) * n_in
            shifted = taps if off == 0 else pltpu.roll(taps, taps.shape[1] - off, axis=1)
            shifted = shifted.astype(BF16)
            for a in range(n_tiles):
                s = a * per_tile + b
                lo = (n_tiles - 1 - a) * LANES
                m_ref[s * n_in:(s + 1) * n_in, :] = shifted[:, lo:lo + v]
        uv = u_ref[0]
        sl_ref[...] = _dot_tn(uv, pin_ref[0])
        mr, mi = mu_ref[0, 0:1, :], mu_ref[0, 1:2, :]
        sub = lax.broadcasted_iota(jnp.int32, (SUBLANES, half), 0)

        def step(k, carry):
            out = []
            for bi in range(bsz):
                fr, fi, br, bim = carry[bi]
                rows_f = pl.ds(pl.multiple_of(bi * nch + k * SUBLANES, SUBLANES), SUBLANES)
                rows_b = pl.ds(pl.multiple_of(bi * nch + nch - (k + 1) * SUBLANES, SUBLANES), SUBLANES)
                loc_f, loc_b = sl_ref[rows_f, :], sl_ref[rows_b, :]
                tfr = tfi = tbr = tbi = jnp.zeros((SUBLANES, half), F32)
                for i in range(SUBLANES):
                    ib = SUBLANES - 1 - i
                    tfr, tfi = jnp.where(sub == i, fr, tfr), jnp.where(sub == i, fi, tfi)
                    tbr, tbi = jnp.where(sub == ib, br, tbr), jnp.where(sub == ib, bim, tbi)
                    lfr, lfi = loc_f[i:i + 1, :half], loc_f[i:i + 1, half:]
                    lbr, lbi = loc_b[ib:ib + 1, :half], loc_b[ib:ib + 1, half:]
                    fr, fi = mr * fr - mi * fi + lfr, mr * fi + mi * fr + lfi
                    br, bim = mr * br - mi * bim + lbr, mr * bim + mi * br + lbi
                spf_ref[rows_f, :half] = tfr
                spf_ref[rows_f, half:] = tfi
                spb_ref[rows_b, :half] = tbr
                spb_ref[rows_b, half:] = tbi
                out.append((fr, fi, br, bim))
            return tuple(out)

        zero = jnp.zeros((1, half), F32)
        lax.fori_loop(0, nch // SUBLANES, step, tuple((zero, zero, zero, zero) for _ in range(bsz)))
        lane = lax.broadcasted_iota(jnp.int32, (rows, w4), 1)
        fwd_lane = (lane & (half - 1)) < (half // 2)
        carry_in = jnp.where(fwd_lane, spf_ref[...], spb_ref[...]).astype(BF16)
        y = _dot(m_ref[...], uv) + _dot_nt(pout_ref[0], carry_in)
        z_ref[0] = jax.nn.gelu(y + pltpu.repeat(skip_ref[0], tc, axis=0) * uv.astype(F32)).astype(BF16)

    def spec(a):
        nd = len(a.shape)
        return pl.BlockSpec((1,) + a.shape[1:], lambda g: (g,) + (0,) * (nd - 1))

    out = jax.ShapeDtypeStruct((groups, v, rows), BF16)
    return pl.pallas_call(
        body, grid=(groups,),
        in_specs=[spec(ut), spec(drow), spec(p_in), spec(p_out_t), spec(mu), spec(skip)],
        out_specs=spec(out), out_shape=out,
        scratch_shapes=[pltpu.VMEM((v, v), BF16), pltpu.VMEM((rows, w4), F32),
                        pltpu.VMEM((rows, w4), F32), pltpu.VMEM((rows, w4), F32)],
        compiler_params=_params("parallel"), name="s5_toeplitz")(ut, drow, p_in, p_out_t, mu, skip)


def _s5_mix(u2, bsz, seq, e, b_re, b_im, c_re, c_im, d_skip, a_re_f, a_im_f, a_re_b, a_im_b, ls_f, ls_b):
    groups, states, n_in = b_re.shape
    tc = S5_CHUNK
    nch = seq // tc
    flat = lambda t: t.astype(F32).reshape(groups * states // LANES, LANES)
    bt = lambda t: t.astype(F32).transpose(2, 0, 1).reshape(n_in, groups * states // LANES, LANES)
    cat = lambda *ts: jnp.concatenate(ts, axis=-1)
    flip = lambda t: jnp.flip(t, axis=1)

    def direction(a_re, a_im, ls):
        pr, pi, bbr, bbi = _s5_powers(flat(a_re), flat(a_im), flat(ls), bt(b_re), bt(b_im), tc + 1)
        per_group = lambda t: t.reshape(t.shape[0], groups, states).transpose(1, 0, 2)
        return per_group(pr), per_group(pi), per_group(bbr), per_group(bbi)

    pfr, pfi, bfr, bfi = direction(a_re_f, a_im_f, ls_f)
    pbr, pbi, bbr, bbi = direction(a_re_b, a_im_b, ls_b)
    cr, ci = c_re.astype(F32), c_im.astype(F32)

    lr_in = cat(flip(pfr[:, :tc]), pbr[:, :tc], flip(pfr[:, :tc]), pbr[:, :tc])
    li_in = cat(flip(pfi[:, :tc]), pbi[:, :tc], flip(pfi[:, :tc]), pbi[:, :tc])
    lr_out = cat(pfr[:, 1:], flip(pbr[:, 1:]), pfr[:, 1:], flip(pbr[:, 1:]))
    li_out = cat(pfi[:, 1:], flip(pbi[:, 1:]), pfi[:, 1:], flip(pbi[:, 1:]))
    xa, xb = cat(bfr, bbr, bfi, bbi), cat(-bfi, -bbi, bfr, bbr)
    ca, cb = cat(cr, cr, -ci, -ci), cat(-ci, -ci, -cr, -cr)
    lhs_b = jnp.pad(cat(flip(pbr[:, :tc]), flip(pbi[:, :tc])), ((0, 0), (0, tc), (0, 0)))
    lhs_f = jnp.pad(cat(pfr[:, :tc], pfi[:, :tc]), ((0, 0), (tc - 1, 1), (0, 0)))
    ca2, cb2 = cat(cr, -ci), cat(-ci, -cr)
    p_in, p_out_t, dtab = _s5_tables(lr_in, li_in, xa, xb, lr_out, li_out, ca, cb, lhs_b, lhs_f,
                                     cat(bfr, bfr), cat(bfi, bfi), cat(bbr, bbr), cat(bbi, bbi), ca2, cb2)
    taps = jnp.flip(dtab.reshape(groups, 2 * tc, n_in, n_in)[:, :2 * tc - 1], axis=1)
    taps = jnp.pad(taps, ((0, 0), (0, 1), (0, 0), (0, 0)))
    drow = taps.transpose(0, 3, 1, 2).reshape(groups, n_in, 2 * tc * n_in)
    mu = jnp.stack([cat(pfr[:, tc], pbr[:, tc]), cat(pfi[:, tc], pbi[:, tc])], axis=1)

    rows = bsz * nch
    skip = jnp.broadcast_to(d_skip.astype(F32)[:, :, None], (groups, n_in, rows))
    zt = _s5_toeplitz(_s5_pack(u2, tc), drow, p_in, p_out_t, mu, skip, bsz)
    return _s5_unpack(zt, tc)


def _s5_glu(z, g, w_glu, layer, b_glu, seq):
    e = z.shape[1]

    def specs(tm, tn):
        tile = pl.BlockSpec((tm, tn), lambda j, i: (i, j))
        return [tile, tile, pl.BlockSpec((1, tn), lambda j, i: (0, j))]

    def epilogue(acc, z_ref, g_ref, b_ref):
        return z_ref[...].astype(F32) * jax.nn.sigmoid(acc + b_ref[...]) * _silu(g_ref[...])

    return _matmul("s5_glu", z, w_glu, layer, 0, e, [z, g, b_glu.reshape(1, e)], specs, epilogue, BF16, seq)


def _s5_layer(x2, h, layer, bsz, seq, gate, w_in, b_re, b_im, c_re, c_im, d_skip,
              a_re_f, a_im_f, a_re_b, a_im_b, log_step_f, log_step_b, w_glu, b_glu, w_out):
    e = w_out.shape[1]
    u2 = _matmul("s5_in_u", h, w_in, layer, 0, e, [], _no_specs, _identity, BF16, seq)
    g = _matmul("s5_in_gate", h, w_in, layer, e, e, [], _no_specs, _identity, F32, seq)
    pick = lambda t: t[layer]
    z = _s5_mix(u2, bsz, seq, e, pick(b_re), pick(b_im), pick(c_re), pick(c_im), pick(d_skip), pick(a_re_f),
                pick(a_im_f), pick(a_re_b), pick(a_im_b), pick(log_step_f), pick(log_step_b))
    a = _s5_glu(z, g, w_glu, layer, b_glu[layer].astype(F32), seq)
    return _out_proj("s5_out", a, w_out, layer, x2, gate, seq)


def _gla_core(proj, lowrank, w2_f, bg_f, w2_b, bg_b, bsz, seq, e, chunks_per_step=4):
    heads = e // GLA_DV
    dk, dv, ch = GLA_DK, GLA_DV, GLA_CHUNK
    rt = min(chunks_per_step * ch, seq)
    ncs = rt // ch
    nb = seq // rt
    q_scale = dk ** -0.5

    def prepare(forward, q_ref, k_ref, v_ref, lr_ref, w2_ref, bg_ref):
        row = lax.broadcasted_iota(jnp.int32, (rt, rt), 0)
        col = lax.broadcasted_iota(jnp.int32, (rt, rt), 1)
        shift = ch.bit_length() - 1
        same = (row >> shift) == (col >> shift)
        if forward:
            cum_mask = same & (col <= row)
            score_mask = cum_mask
        else:
            cum_mask = same & (col >= row)
            score_mask = same & (col > row)
        gk = _log_sigmoid(_dot(lr_ref[...].astype(BF16), w2_ref[...]) + bg_ref[...]) / GLA_GATE_NORM
        tri = jnp.where(cum_mask, 1.0, 0.0).astype(BF16)
        g1, g2, g3 = _split3(gk)
        bcum = _dot(tri, g1) + _dot(tri, g2) + _dot(tri, g3)
        k = k_ref[...]
        v = v_ref[...].astype(BF16)
        q_s = (q_ref[...] * q_scale * jnp.exp(bcum)).astype(BF16)
        k_s = (k * jnp.exp(-bcum)).astype(BF16)
        scores = jnp.where(score_mask, _dot_nt(q_s, k_s), 0.0)
        o_intra = _dot(scores.astype(BF16), v)
        return q_s, k, v, bcum, o_intra

    def advance(forward, c, prep, o_ref, st_ref):
        q_s, k, v, bcum, o_intra = prep
        rows = slice(c * ch, (c + 1) * ch)
        last = c * ch + ch - 1 if forward else c * ch
        b_last = bcum[last:last + 1, :]
        state = st_ref[...]
        o_ref[rows, :] = o_intra[rows] + _dot_nt(q_s[rows], state.astype(BF16))
        k_end = (k[rows] * jnp.exp(b_last - bcum[rows])).astype(BF16)
        st_ref[...] = state * jnp.exp(b_last) + _dot_tn(v[rows], k_end)

    def body(qf, kf, vf, lf, qb, kb, vb, lb, w2f, bgf, w2b, bgb, of, ob, stf, stb):
        @pl.when(pl.program_id(2) == 0)
        def _():
            stf[...] = jnp.zeros_like(stf)
            stb[...] = jnp.zeros_like(stb)

        prep_f = prepare(True, qf, kf, vf, lf, w2f, bgf)
        prep_b = prepare(False, qb, kb, vb, lb, w2b, bgb)
        for c in range(ncs):
            advance(True, c, prep_f, of, stf)
            advance(False, ncs - 1 - c, prep_b, ob, stb)

    def tok(forward):
        return (lambda b, h, n: b * nb + n) if forward else (lambda b, h, n: b * nb + nb - 1 - n)

    def specs(forward):
        t = tok(forward)
        return [pl.BlockSpec((rt, dk), lambda b, h, n: (t(b, h, n), h)),
                pl.BlockSpec((rt, dk), lambda b, h, n: (t(b, h, n), heads + h)),
                pl.BlockSpec((rt, dv), lambda b, h, n: (t(b, h, n), heads + h)),
                pl.BlockSpec((rt, LANES), lambda b, h, n: (t(b, h, n), 0))]

    w2_spec = pl.BlockSpec((LANES, dk), lambda b, h, n: (0, h))
    bg_spec = pl.BlockSpec((1, dk), lambda b, h, n: (0, h))
    out = jax.ShapeDtypeStruct((bsz * seq, e), F32)
    tf, tbk = tok(True), tok(False)
    return pl.pallas_call(
        body,
        grid=(bsz, heads, nb),
        in_specs=specs(True) + specs(False) + [w2_spec, bg_spec, w2_spec, bg_spec],
        out_specs=(pl.BlockSpec((rt, dv), lambda b, h, n: (tf(b, h, n), h)),
                   pl.BlockSpec((rt, dv), lambda b, h, n: (tbk(b, h, n), h))),
        out_shape=(out, out),
        scratch_shapes=[pltpu.VMEM((dv, dk), F32), pltpu.VMEM((dv, dk), F32)],
        compiler_params=_params("parallel", "parallel", "arbitrary"),
        name="gla_core",
    )(proj, proj, proj, lowrank, proj, proj, proj, lowrank, w2_f, bg_f.reshape(1, -1),
      w2_b, bg_b.reshape(1, -1))


def _gla_gate(of, ob, proj, norm_w, tm=1024):
    m, e = of.shape
    heads = e // GLA_DV
    tm = _tile(m, tm)
    g_col0 = (proj.shape[1] - e) // GLA_DV

    def body(of_ref, ob_ref, g_ref, nw_ref, o_ref):
        o = of_ref[...] + ob_ref[...]
        o = o * lax.rsqrt(jnp.mean(o * o, axis=-1, keepdims=True) + EPS) * nw_ref[...]
        o_ref[...] = (o * _silu(g_ref[...])).astype(BF16)

    tile = pl.BlockSpec((tm, GLA_DV), lambda i, h: (i, h))
    return pl.pallas_call(
        body, grid=(m // tm, heads),
        in_specs=[tile, tile, pl.BlockSpec((tm, GLA_DV), lambda i, h: (i, g_col0 + h)),
                  pl.BlockSpec((1, GLA_DV), lambda i, h: (0, 0))],
        out_specs=tile, out_shape=jax.ShapeDtypeStruct((m, e), BF16),
        compiler_params=_params("parallel", "parallel"), name="gla_gate")(of, ob, proj, norm_w.reshape(1, GLA_DV))


def _gla_layer(x2, h, layer, bsz, seq, gate, w_in, w1_f, w2_f, bg_f, w1_b, w2_b, bg_b, gla_norm_w, w_out):
    e = w_out.shape[1]
    d = w_in.shape[1]
    rank = w1_f.shape[2]
    proj = _matmul("gla_in", h, w_in, layer, 0, w_in.shape[2], [], _no_specs, _identity, F32, seq)
    w1 = jnp.zeros((1, d, LANES), F32).at[0, :, :rank].set(w1_f[layer].astype(F32))
    w1 = w1.at[0, :, rank:2 * rank].set(w1_b[layer].astype(F32))
    lowrank = _matmul("gla_in_gate_rank", h, w1, 0, 0, LANES, [], _no_specs, _identity, F32, seq)
    kw = w2_f.shape[2]
    w2f = jnp.zeros((LANES, kw), BF16).at[:rank].set(w2_f[layer].astype(BF16))
    w2b = jnp.zeros((LANES, kw), BF16).at[rank:2 * rank].set(w2_b[layer].astype(BF16))
    of, ob = _gla_core(proj, lowrank, w2f, bg_f[layer].astype(F32), w2b, bg_b[layer].astype(F32), bsz, seq, e)
    a = _gla_gate(of, ob, proj, gla_norm_w[layer].astype(F32))
    return _out_proj("gla_out", a, w_out, layer, x2, gate, seq)


def _final_norm(x2, w, tm=512):
    m, d = x2.shape
    tm = _tile(m, tm)

    def body(x_ref, w_ref, o_ref):
        xv = x_ref[...]
        o_ref[...] = xv * lax.rsqrt(jnp.mean(xv * xv, axis=-1, keepdims=True) + EPS) * w_ref[...]

    return pl.pallas_call(
        body, grid=(m // tm,),
        in_specs=[pl.BlockSpec((tm, d), lambda i: (i, 0)), pl.BlockSpec((1, d), lambda i: (0, 0))],
        out_specs=pl.BlockSpec((tm, d), lambda i: (i, 0)),
        out_shape=jax.ShapeDtypeStruct((m, d), F32),
        compiler_params=_params("parallel"),
        name="final_norm",
    )(x2, w.reshape(1, d).astype(F32))


def kernel(x, c, mod_w, mod_b, norm_w, na_w_in, na_rpb, na_w_out, s5_w_in, s5_b_re, s5_b_im, s5_c_re, s5_c_im, s5_d, s5_a_re_fwd, s5_a_im_fwd, s5_a_re_bwd, s5_a_im_bwd, s5_log_step_fwd, s5_log_step_bwd, s5_w_glu, s5_b_glu, s5_w_out, gla_w_in, gla_gk_w1_fwd, gla_gk_w2_fwd, gla_gk_b_fwd, gla_gk_w1_bwd, gla_gk_w2_bwd, gla_gk_b_bwd, gla_norm_w, gla_w_out, final_norm_w):
    bsz, seq, d = x.shape
    depth = mod_w.shape[0]
    mod = _modulation(c.astype(F32), mod_w, mod_b.astype(F32))
    x2 = x.astype(F32).reshape(bsz * seq, d)
    for i in range(depth):
        kind, j = i % 3, i // 3
        shift, scale, gate = mod[i, :, :d], mod[i, :, d:2 * d], mod[i, :, 2 * d:]
        h = _norm_modulate(x2, seq, norm_w[i].astype(F32), scale, shift)
        if kind == 0:
            x2 = _na_layer(x2, h, j, bsz, seq, gate, na_w_in, na_rpb, na_w_out)
        elif kind == 1:
            x2 = _s5_layer(x2, h, j, bsz, seq, gate, s5_w_in, s5_b_re, s5_b_im, s5_c_re, s5_c_im, s5_d,
                           s5_a_re_fwd, s5_a_im_fwd, s5_a_re_bwd, s5_a_im_bwd, s5_log_step_fwd,
                           s5_log_step_bwd, s5_w_glu, s5_b_glu, s5_w_out)
        else:
            x2 = _gla_layer(x2, h, j, bsz, seq, gate, gla_w_in, gla_gk_w1_fwd, gla_gk_w2_fwd, gla_gk_b_fwd,
                            gla_gk_w1_bwd, gla_gk_w2_bwd, gla_gk_b_bwd, gla_norm_w, gla_w_out)
    return _final_norm(x2, final_norm_w).reshape(bsz, seq, d)
```

```python
import jax
import jax.numpy as jnp
from jax import lax
from jax.experimental import pallas as pl
from jax.experimental.pallas import tpu as pltpu

EPS = 1e-6
GRID_W = 64
NA_HEAD_DIM = 128
NA_ROWS = 8
NA_COLS = 16
NA_ROW_BLOCK = 8
NA_HEADS_PER_STEP = 2
NA_PIPELINE_GROUP = 8
S5_GROUP = 16
S5_STATE = 64
GLA_DV = 512
GLA_DK = 256
GLA_GATE_NORM = 16.0
GLA_CHUNK = 64
GLA_HEADS_PER_STEP = 2
NEG_INF = -1e30
LOG2_E = 1.4426950408889634

V7X_VMEM_BYTES = 64 * 1024 * 1024
VMEM_LIMIT = V7X_VMEM_BYTES - 8 * 1024 * 1024
LANES = 128
SUBLANES = 8
BF16 = jnp.bfloat16
F32 = jnp.float32


def _params(*sem):
    return pltpu.CompilerParams(dimension_semantics=sem, vmem_limit_bytes=VMEM_LIMIT)


def _silu(x):
    return x * jax.nn.sigmoid(x)


def _log_sigmoid(x):
    return jnp.minimum(x, 0.0) - jnp.log(1.0 + jnp.exp(-jnp.abs(x)))


def _dot(a, b):
    return jnp.dot(a, b, preferred_element_type=F32)


def _dot_nt(a, b):
    return lax.dot_general(a, b, (((1,), (1,)), ((), ())), preferred_element_type=F32)


def _dot_tn(a, b):
    return lax.dot_general(a, b, (((0,), (0,)), ((), ())), preferred_element_type=F32)


def _split3(x):
    p1 = x.astype(BF16)
    r1 = x - p1.astype(F32)
    p2 = r1.astype(BF16)
    p3 = (r1 - p2.astype(F32)).astype(BF16)
    return p1, p2, p3


def _dot_nt_f32(a, b):
    a1, a2, a3 = _split3(a)
    b1, b2, b3 = _split3(b)
    return (_dot_nt(a1, b1) + (_dot_nt(a1, b2) + _dot_nt(a2, b1))
            + (_dot_nt(a1, b3) + _dot_nt(a2, b2) + _dot_nt(a3, b1)))


def _tile(n, want):
    t = min(want, n)
    assert n % t == 0
    return t


def _matmul(name, a, w, layer, col0, n, e_ins, e_specs, epilogue, out_dtype, seq, tm=1024, tn=512):
    m, k = a.shape
    tm = _tile(seq, tm)
    tn = _tile(n, tn)
    assert col0 % tn == 0
    jb = col0 // tn
    n_e = len(e_ins)

    def body(a_ref, w_ref, *rest):
        e_refs, o_ref, wb_ref = rest[:n_e], rest[n_e], rest[n_e + 1]

        @pl.when(pl.program_id(1) == 0)
        def _():
            wb_ref[...] = w_ref[...].astype(BF16)

        o_ref[...] = epilogue(_dot(a_ref[...], wb_ref[...]), *e_refs).astype(out_dtype)

    return pl.pallas_call(
        body,
        grid=(n // tn, m // tm),
        in_specs=[pl.BlockSpec((tm, k), lambda j, i: (i, 0)),
                  pl.BlockSpec((None, k, tn), lambda j, i: (layer, 0, jb + j))] + list(e_specs(tm, tn)),
        out_specs=pl.BlockSpec((tm, tn), lambda j, i: (i, j)),
        out_shape=jax.ShapeDtypeStruct((m, n), out_dtype),
        scratch_shapes=[pltpu.VMEM((k, tn), BF16)],
        compiler_params=_params("parallel", "arbitrary"),
        name=name,
    )(a, w, *e_ins)


def _no_specs(tm, tn):
    return []


def _identity(acc):
    return acc


def _out_proj(name, a, w, layer, x2, gate, seq):
    bsz, d = gate.shape

    def specs(tm, tn):
        tiles_per_seq = seq // tm
        return [pl.BlockSpec((tm, tn), lambda j, i: (i, j)),
                pl.BlockSpec((1, 1, tn), lambda j, i: (i // tiles_per_seq, 0, j))]

    def epilogue(acc, x_ref, gate_ref):
        return x_ref[...] + gate_ref[0] * acc

    return _matmul(name, a, w, layer, 0, d, [x2, gate.reshape(bsz, 1, d)], specs, epilogue, F32, seq)


def _modulation(c, mod_w, mod_b):
    depth, d, n = mod_w.shape
    bsz = c.shape[0]
    c_pad = jnp.zeros((SUBLANES, d), F32).at[:bsz].set(c)
    tn = _tile(n, 512)

    def body(c_ref, w_ref, b_ref, o_ref):
        act = _silu(c_ref[...]).astype(BF16)
        o_ref[0] = _dot(act, w_ref[0].astype(BF16)) + b_ref[0]

    out = pl.pallas_call(
        body,
        grid=(depth, n // tn),
        in_specs=[pl.BlockSpec((SUBLANES, d), lambda i, j: (0, 0)),
                  pl.BlockSpec((1, d, tn), lambda i, j: (i, 0, j)),
                  pl.BlockSpec((1, 1, tn), lambda i, j: (i, 0, j))],
        out_specs=pl.BlockSpec((1, SUBLANES, tn), lambda i, j: (i, 0, j)),
        out_shape=jax.ShapeDtypeStruct((depth, SUBLANES, n), F32),
        compiler_params=_params("parallel", "parallel"),
        name="modulation",
    )(c_pad, mod_w, mod_b.reshape(depth, 1, n))
    return out[:, :bsz]


def _norm_modulate(x2, seq, norm_w, scale, shift, tm=512):
    m, d = x2.shape
    tm = _tile(seq, tm)
    tiles_per_seq = seq // tm
    bsz = m // seq

    def body(x_ref, nw_ref, sc_ref, sh_ref, o_ref):
        xv = x_ref[...]
        y = xv * lax.rsqrt(jnp.mean(xv * xv, axis=-1, keepdims=True) + EPS) * nw_ref[...]
        o_ref[...] = (y * (1.0 + sc_ref[0]) + sh_ref[0]).astype(BF16)

    vec_spec = pl.BlockSpec((1, 1, d), lambda i: (i // tiles_per_seq, 0, 0))
    return pl.pallas_call(
        body, grid=(m // tm,),
        in_specs=[pl.BlockSpec((tm, d), lambda i: (i, 0)), pl.BlockSpec((1, d), lambda i: (0, 0)),
                  vec_spec, vec_spec],
        out_specs=pl.BlockSpec((tm, d), lambda i: (i, 0)),
        out_shape=jax.ShapeDtypeStruct((m, d), BF16),
        compiler_params=_params("parallel"),
        name="norm_modulate",
    )(x2, norm_w.reshape(1, d), scale.reshape(bsz, 1, d), shift.reshape(bsz, 1, d))


def _na_bias_table(rpb):
    w = GRID_W
    qc = jnp.arange(w)
    kc = jnp.arange(w)
    col_start = jnp.clip(qc - NA_COLS // 2, 0, w - NA_COLS)
    col_ok = (kc[None, :] >= col_start[:, None]) & (kc[None, :] < col_start[:, None] + NA_COLS)
    dc = kc[None, :] - qc[:, None] + NA_COLS - 1
    onehot = ((dc[None] == jnp.arange(2 * NA_COLS - 1)[:, None, None]) & col_ok[None]).astype(F32)
    cexp = jnp.einsum('hdc,cqk->hdqk', rpb.astype(F32), onehot, precision=lax.Precision.HIGHEST)
    cexp = jnp.where(col_ok[None, None], cexp * LOG2_E, NEG_INF)
    tables = [jnp.concatenate([cexp[:, t + m] for m in range(NA_ROWS)], axis=-1) for t in range(NA_ROWS)]
    return jnp.stack(tables, axis=1)


def _na_attention(qkv, g, bias, bsz, seq):
    e = g.shape[1]
    heads = e // NA_HEAD_DIM
    rows = seq // GRID_W
    tq = NA_ROW_BLOCK * GRID_W
    nblk = rows // NA_ROW_BLOCK
    assert nblk >= 2 and rows % NA_ROW_BLOCK == 0
    half = tq // 2
    hp = NA_HEADS_PER_STEP
    hw = hp * NA_HEAD_DIM
    win = NA_ROWS * GRID_W
    half_rows = NA_ROWS // 2
    scale = NA_HEAD_DIM ** -0.5 * LOG2_E
    assert heads % hp == 0 and NA_ROW_BLOCK == NA_ROWS

    def body(q_ref, kp_ref, kc_ref, kn_ref, vp_ref, vc_ref, vn_ref, g_ref, b_ref, o_ref):
        def window(above_ref, cur_ref, below_ref, keys, lanes):
            parts = []
            for ref, lo, hi in ((above_ref, 0, half), (cur_ref, half, half + tq), (below_ref, half + tq, 2 * tq)):
                a, b = max(keys.start, lo), min(keys.stop, hi)
                if a < b:
                    parts.append(ref[a - lo:b - lo, lanes])
            return parts[0] if len(parts) == 1 else jnp.concatenate(parts, axis=0)

        def attend(r0):
            def place(qr, h):
                r = r0 + qr
                row_start = min(max(r - half_rows, 0), rows - NA_ROWS)
                start = (row_start - r0 + half_rows) * GRID_W
                return (slice(qr * GRID_W, (qr + 1) * GRID_W), slice(h * NA_HEAD_DIM, (h + 1) * NA_HEAD_DIM),
                        slice(start, start + win), row_start - r + NA_ROWS - 1)

            def scores(qr, h):
                q_rows, lanes, keys, table = place(qr, h)
                k = window(kp_ref, kc_ref, kn_ref, keys, lanes)
                return _dot_nt(q_ref[q_rows, lanes], k) * scale + b_ref[h, table]

            def finish(qr, h, s):
                q_rows, lanes, keys, _ = place(qr, h)
                p = jnp.exp2(s - jnp.max(s, axis=-1, keepdims=True))
                denom = jnp.sum(p, axis=-1, keepdims=True)
                o = _dot(p.astype(BF16), window(vp_ref, vc_ref, vn_ref, keys, lanes)) / denom
                o_ref[q_rows, lanes] = (o * _silu(g_ref[q_rows, lanes])).astype(BF16)

            work = [(qr, h) for qr in range(NA_ROW_BLOCK) for h in range(hp)]
            groups = [work[i:i + NA_PIPELINE_GROUP] for i in range(0, len(work), NA_PIPELINE_GROUP)]
            pending = [scores(*w) for w in groups[0]]
            for gi, grp in enumerate(groups):
                ahead = [scores(*w) for w in groups[gi + 1]] if gi + 1 < len(groups) else []
                for w, s in zip(grp, pending):
                    finish(*w, s)
                pending = ahead

        blk = pl.program_id(2)
        pl.when(blk == 0)(lambda: attend(0))
        pl.when(blk == nblk - 1)(lambda: attend(rows - NA_ROW_BLOCK))
        pl.when((blk > 0) & (blk < nblk - 1))(lambda: attend(NA_ROW_BLOCK))

    def cur(col0):
        return pl.BlockSpec((tq, hw), lambda b, h, i: (b * nblk + i, col0 + h))

    def above(col0):
        return pl.BlockSpec((half, hw), lambda b, h, i: (jnp.maximum(2 * (b * nblk + i) - 1, 0), col0 + h))

    def below(col0):
        last = 2 * bsz * nblk - 1
        return pl.BlockSpec((half, hw), lambda b, h, i: (jnp.minimum(2 * (b * nblk + i) + 2, last), col0 + h))

    kcol, vcol = heads // hp, 2 * heads // hp
    return pl.pallas_call(
        body,
        grid=(bsz, heads // hp, nblk),
        in_specs=[cur(0),
                  above(kcol), cur(kcol), below(kcol),
                  above(vcol), cur(vcol), below(vcol),
                  cur(0),
                  pl.BlockSpec((hp, NA_ROWS, GRID_W, win), lambda b, h, i: (h, 0, 0, 0))],
        out_specs=cur(0),
        out_shape=jax.ShapeDtypeStruct((bsz * seq, e), BF16),
        compiler_params=_params("parallel", "parallel", "arbitrary"),
        name="na_attention",
    )(qkv, qkv, qkv, qkv, qkv, qkv, qkv, g, bias)


def _na_layer(x2, h, layer, bsz, seq, gate, w_in, rpb, w_out):
    e = w_out.shape[1]
    qkv = _matmul("na_in_qkv", h, w_in, layer, 0, 3 * e, [], _no_specs, _identity, BF16, seq)
    g = _matmul("na_in_gate", h, w_in, layer, 3 * e, e, [], _no_specs, _identity, F32, seq)
    a = _na_attention(qkv, g, _na_bias_table(rpb[layer]), bsz, seq)
    return _out_proj("na_out", a, w_out, layer, x2, gate, seq)


S5_CHUNK = 64


def _s5_powers(a_re, a_im, log_step, b_re_t, b_im_t, n_pow):
    n_in, rows, lanes = b_re_t.shape

    def body(are_ref, aim_ref, ls_ref, bre_ref, bim_ref, pre_ref, pim_ref, bbre_ref, bbim_ref):
        dt = jnp.exp(ls_ref[...])
        are, aim = are_ref[...], aim_ref[...]
        mag = jnp.exp(are * dt)
        lam_re = mag * jnp.cos(aim * dt)
        lam_im = mag * jnp.sin(aim * dt)
        den = are * are + aim * aim
        nr = lam_re - 1.0
        r_re = (nr * are + lam_im * aim) / den
        r_im = (lam_im * are - nr * aim) / den
        for i in range(n_in):
            bbre_ref[i] = r_re * bre_ref[i] - r_im * bim_ref[i]
            bbim_ref[i] = r_re * bim_ref[i] + r_im * bre_ref[i]

        def power(tau, carry):
            t = jnp.asarray(tau, F32)
            dtv = jnp.exp(ls_ref[...])
            m = jnp.exp(t * (are_ref[...] * dtv))
            ang = t * (aim_ref[...] * dtv)
            pre_ref[tau] = m * jnp.cos(ang)
            pim_ref[tau] = m * jnp.sin(ang)
            return carry

        lax.fori_loop(0, n_pow, power, 0)

    pw = jax.ShapeDtypeStruct((n_pow, rows, lanes), F32)
    bb = jax.ShapeDtypeStruct((n_in, rows, lanes), F32)
    return pl.pallas_call(body, out_shape=(pw, pw, bb, bb), name="s5_powers",
                          compiler_params=pltpu.CompilerParams(vmem_limit_bytes=VMEM_LIMIT),
                          )(a_re, a_im, log_step, b_re_t, b_im_t)


def _s5_tables(lr_in, li_in, xa, xb, lr_out, li_out, ca, cb,
               lhs_b, lhs_f, br2_f, bi2_f, br2_b, bi2_b, ca2, cb2):
    groups, tc, width = lr_in.shape
    n_in = xa.shape[1]

    def body(lri, lii, xa_r, xb_r, lro, lio, ca_r, cb_r, lb, lf, brf, bif, brb, bib, ca2_r, cb2_r,
             pin_ref, pout_ref, dtab_ref):
        for s in range(tc):
            rows = slice(s * n_in, (s + 1) * n_in)
            pin_ref[0, rows, :] = (lri[0, s:s + 1, :] * xa_r[0] + lii[0, s:s + 1, :] * xb_r[0]).astype(BF16)
            pout_ref[0, rows, :] = (lro[0, s:s + 1, :] * ca_r[0] + lio[0, s:s + 1, :] * cb_r[0]).astype(BF16)

        def gcat(br, bi):
            return jnp.concatenate(
                [br[0, j:j + 1, :] * ca2_r[0] + bi[0, j:j + 1, :] * cb2_r[0] for j in range(n_in)], axis=0)

        dtab_ref[0] = _dot_nt_f32(lb[0], gcat(brb, bib)) + _dot_nt_f32(lf[0], gcat(brf, bif))

    def spec(a):
        return pl.BlockSpec((1,) + a.shape[1:], lambda g: (g, 0, 0))

    ins = [lr_in, li_in, xa, xb, lr_out, li_out, ca, cb, lhs_b, lhs_f, br2_f, bi2_f, br2_b, bi2_b, ca2, cb2]
    pshape = jax.ShapeDtypeStruct((groups, tc * n_in, width), BF16)
    dshape = jax.ShapeDtypeStruct((groups, 2 * tc, n_in * n_in), F32)
    return pl.pallas_call(
        body, grid=(groups,), in_specs=[spec(a) for a in ins],
        out_specs=(spec(pshape), spec(pshape), spec(dshape)),
        out_shape=(pshape, pshape, dshape),
        compiler_params=_params("parallel"), name="s5_tables")(*ins)


def _s5_pack(u2, tc, rb=128):
    m, e = u2.shape
    r_total = m // tc
    rb = _tile(r_total, rb)
    gpl = LANES // S5_GROUP

    def body(x_ref, o_ref, xf_ref):
        xf_ref[...] = x_ref[...].astype(F32)
        for s in range(tc):
            slab_t = xf_ref[:, s, :].T
            for gl in range(gpl):
                o_ref[gl, s * S5_GROUP:(s + 1) * S5_GROUP, :] = (
                    slab_t[gl * S5_GROUP:(gl + 1) * S5_GROUP, :].astype(BF16))

    return pl.pallas_call(
        body, grid=(r_total // rb, e // LANES),
        in_specs=[pl.BlockSpec((rb, tc, LANES), lambda i, j: (i, 0, j))],
        out_specs=pl.BlockSpec((gpl, tc * S5_GROUP, rb), lambda i, j: (j, 0, i)),
        out_shape=jax.ShapeDtypeStruct((e // S5_GROUP, tc * S5_GROUP, r_total), BF16),
        scratch_shapes=[pltpu.VMEM((rb, tc, LANES), F32)],
        compiler_params=_params("parallel", "parallel"), name="s5_pack")(u2.reshape(r_total, tc, e))


def _s5_unpack(zt, tc, rb=128):
    groups, v, r_total = zt.shape
    e = groups * S5_GROUP
    rb = _tile(r_total, rb)
    gpl = LANES // S5_GROUP

    def body(z_ref, o_ref, of_ref):
        for t in range(tc):
            rows = slice(t * S5_GROUP, (t + 1) * S5_GROUP)
            slab_t = jnp.concatenate([z_ref[gl, rows, :].astype(F32) for gl in range(gpl)], axis=0)
            of_ref[:, t, :] = slab_t.T
        o_ref[...] = of_ref[...].astype(BF16)

    out = pl.pallas_call(
        body, grid=(r_total // rb, e // LANES),
        in_specs=[pl.BlockSpec((gpl, v, rb), lambda i, j: (j, 0, i))],
        out_specs=pl.BlockSpec((rb, tc, LANES), lambda i, j: (i, 0, j)),
        out_shape=jax.ShapeDtypeStruct((r_total, tc, e), BF16),
        scratch_shapes=[pltpu.VMEM((rb, tc, LANES), F32)],
        compiler_params=_params("parallel", "parallel"), name="s5_unpack")(zt)
    return out.reshape(r_total * tc, e)


def _s5_toeplitz(ut, drow, p_in, p_out_t, mu, skip, bsz):
    groups, v, rows = ut.shape
    n_in = drow.shape[1]
    tc = v // n_in
    per_tile = LANES // n_in
    n_tiles = tc // per_tile
    nch = rows // bsz
    w4 = p_in.shape[2]
    half = w4 // 2

    def body(u_ref, d_ref, pin_ref, pout_ref, mu_ref, skip_ref, z_ref, m_ref, sl_ref, spf_ref, spb_ref):
        taps = d_ref[0]
        for b in range(per_tile):
            off = (per_tile - 1 - b) * n_in
            shifted = taps if off == 0 else pltpu.roll(taps, taps.shape[1] - off, axis=1)
            shifted = shifted.astype(BF16)
            for a in range(n_tiles):
                s = a * per_tile + b
                lo = (n_tiles - 1 - a) * LANES
                m_ref[s * n_in:(s + 1) * n_in, :] = shifted[:, lo:lo + v]
        uv = u_ref[0]
        sl_ref[...] = _dot_tn(uv, pin_ref[0])
        mr, mi = mu_ref[0, 0:1, :], mu_ref[0, 1:2, :]
        sub = lax.broadcasted_iota(jnp.int32, (SUBLANES, half), 0)

        def step(k, carry):
            out = []
            for bi in range(bsz):
                fr, fi, br, bim = carry[bi]
                rows_f = pl.ds(pl.multiple_of(bi * nch + k * SUBLANES, SUBLANES), SUBLANES)
                rows_b = pl.ds(pl.multiple_of(bi * nch + nch - (k + 1) * SUBLANES, SUBLANES), SUBLANES)
                loc_f, loc_b = sl_ref[rows_f, :], sl_ref[rows_b, :]
                tfr = tfi = tbr = tbi = jnp.zeros((SUBLANES, half), F32)
                for i in range(SUBLANES):
                    ib = SUBLANES - 1 - i
                    tfr, tfi = jnp.where(sub == i, fr, tfr), jnp.where(sub == i, fi, tfi)
                    tbr, tbi = jnp.where(sub == ib, br, tbr), jnp.where(sub == ib, bim, tbi)
                    lfr, lfi = loc_f[i:i + 1, :half], loc_f[i:i + 1, half:]
                    lbr, lbi = loc_b[ib:ib + 1, :half], loc_b[ib:ib + 1, half:]
                    fr, fi = mr * fr - mi * fi + lfr, mr * fi + mi * fr + lfi
                    br, bim = mr * br - mi * bim + lbr, mr * bim + mi * br + lbi
                spf_ref[rows_f, :half] = tfr
                spf_ref[rows_f, half:] = tfi
                spb_ref[rows_b, :half] = tbr
                spb_ref[rows_b, half:] = tbi
                out.append((fr, fi, br, bim))
            return tuple(out)

        zero = jnp.zeros((1, half), F32)
        lax.fori_loop(0, nch // SUBLANES, step, tuple((zero, zero, zero, zero) for _ in range(bsz)))
        lane = lax.broadcasted_iota(jnp.int32, (rows, w4), 1)
        fwd_lane = (lane & (half - 1)) < (half // 2)
        carry_in = jnp.where(fwd_lane, spf_ref[...], spb_ref[...]).astype(BF16)
        y = _dot(m_ref[...], uv) + _dot_nt(pout_ref[0], carry_in)
        z_ref[0] = jax.nn.gelu(y + pltpu.repeat(skip_ref[0], tc, axis=0) * uv.astype(F32)).astype(BF16)

    def spec(a):
        nd = len(a.shape)
        return pl.BlockSpec((1,) + a.shape[1:], lambda g: (g,) + (0,) * (nd - 1))

    out = jax.ShapeDtypeStruct((groups, v, rows), BF16)
    return pl.pallas_call(
        body, grid=(groups,),
        in_specs=[spec(ut), spec(drow), spec(p_in), spec(p_out_t), spec(mu), spec(skip)],
        out_specs=spec(out), out_shape=out,
        scratch_shapes=[pltpu.VMEM((v, v), BF16), pltpu.VMEM((rows, w4), F32),
                        pltpu.VMEM((rows, w4), F32), pltpu.VMEM((rows, w4), F32)],
        compiler_params=_params("parallel"), name="s5_toeplitz")(ut, drow, p_in, p_out_t, mu, skip)


def _s5_mix(u2, bsz, seq, e, b_re, b_im, c_re, c_im, d_skip, a_re_f, a_im_f, a_re_b, a_im_b, ls_f, ls_b):
    groups, states, n_in = b_re.shape
    tc = S5_CHUNK
    nch = seq // tc
    flat = lambda t: t.astype(F32).reshape(groups * states // LANES, LANES)
    bt = lambda t: t.astype(F32).transpose(2, 0, 1).reshape(n_in, groups * states // LANES, LANES)
    cat = lambda *ts: jnp.concatenate(ts, axis=-1)
    flip = lambda t: jnp.flip(t, axis=1)

    def direction(a_re, a_im, ls):
        pr, pi, bbr, bbi = _s5_powers(flat(a_re), flat(a_im), flat(ls), bt(b_re), bt(b_im), tc + 1)
        per_group = lambda t: t.reshape(t.shape[0], groups, states).transpose(1, 0, 2)
        return per_group(pr), per_group(pi), per_group(bbr), per_group(bbi)

    pfr, pfi, bfr, bfi = direction(a_re_f, a_im_f, ls_f)
    pbr, pbi, bbr, bbi = direction(a_re_b, a_im_b, ls_b)
    cr, ci = c_re.astype(F32), c_im.astype(F32)

    lr_in = cat(flip(pfr[:, :tc]), pbr[:, :tc], flip(pfr[:, :tc]), pbr[:, :tc])
    li_in = cat(flip(pfi[:, :tc]), pbi[:, :tc], flip(pfi[:, :tc]), pbi[:, :tc])
    lr_out = cat(pfr[:, 1:], flip(pbr[:, 1:]), pfr[:, 1:], flip(pbr[:, 1:]))
    li_out = cat(pfi[:, 1:], flip(pbi[:, 1:]), pfi[:, 1:], flip(pbi[:, 1:]))
    xa, xb = cat(bfr, bbr, bfi, bbi), cat(-bfi, -bbi, bfr, bbr)
    ca, cb = cat(cr, cr, -ci, -ci), cat(-ci, -ci, -cr, -cr)
    lhs_b = jnp.pad(cat(flip(pbr[:, :tc]), flip(pbi[:, :tc])), ((0, 0), (0, tc), (0, 0)))
    lhs_f = jnp.pad(cat(pfr[:, :tc], pfi[:, :tc]), ((0, 0), (tc - 1, 1), (0, 0)))
    ca2, cb2 = cat(cr, -ci), cat(-ci, -cr)
    p_in, p_out_t, dtab = _s5_tables(lr_in, li_in, xa, xb, lr_out, li_out, ca, cb, lhs_b, lhs_f,
                                     cat(bfr, bfr), cat(bfi, bfi), cat(bbr, bbr), cat(bbi, bbi), ca2, cb2)
    taps = jnp.flip(dtab.reshape(groups, 2 * tc, n_in, n_in)[:, :2 * tc - 1], axis=1)
    taps = jnp.pad(taps, ((0, 0), (0, 1), (0, 0), (0, 0)))
    drow = taps.transpose(0, 3, 1, 2).reshape(groups, n_in, 2 * tc * n_in)
    mu = jnp.stack([cat(pfr[:, tc], pbr[:, tc]), cat(pfi[:, tc], pbi[:, tc])], axis=1)

    rows = bsz * nch
    skip = jnp.broadcast_to(d_skip.astype(F32)[:, :, None], (groups, n_in, rows))
    zt = _s5_toeplitz(_s5_pack(u2, tc), drow, p_in, p_out_t, mu, skip, bsz)
    return _s5_unpack(zt, tc)


def _s5_glu(z, g, w_glu, layer, b_glu, seq):
    e = z.shape[1]

    def specs(tm, tn):
        tile = pl.BlockSpec((tm, tn), lambda j, i: (i, j))
        return [tile, tile, pl.BlockSpec((1, tn), lambda j, i: (0, j))]

    def epilogue(acc, z_ref, g_ref, b_ref):
        return z_ref[...].astype(F32) * jax.nn.sigmoid(acc + b_ref[...]) * _silu(g_ref[...])

    return _matmul("s5_glu", z, w_glu, layer, 0, e, [z, g, b_glu.reshape(1, e)], specs, epilogue, BF16, seq)


def _s5_layer(x2, h, layer, bsz, seq, gate, w_in, b_re, b_im, c_re, c_im, d_skip,
              a_re_f, a_im_f, a_re_b, a_im_b, log_step_f, log_step_b, w_glu, b_glu, w_out):
    e = w_out.shape[1]
    u2 = _matmul("s5_in_u", h, w_in, layer, 0, e, [], _no_specs, _identity, BF16, seq)
    g = _matmul("s5_in_gate", h, w_in, layer, e, e, [], _no_specs, _identity, F32, seq)
    pick = lambda t: t[layer]
    z = _s5_mix(u2, bsz, seq, e, pick(b_re), pick(b_im), pick(c_re), pick(c_im), pick(d_skip), pick(a_re_f),
                pick(a_im_f), pick(a_re_b), pick(a_im_b), pick(log_step_f), pick(log_step_b))
    a = _s5_glu(z, g, w_glu, layer, b_glu[layer].astype(F32), seq)
    return _out_proj("s5_out", a, w_out, layer, x2, gate, seq)


def _gla_core(proj, lowrank, w2_f, bg_f, w2_b, bg_b, bsz, seq, e, chunks_per_step=4):
    heads = e // GLA_DV
    dk, dv, ch = GLA_DK, GLA_DV, GLA_CHUNK
    rt = min(chunks_per_step * ch, seq)
    ncs = rt // ch
    nb = seq // rt
    q_scale = dk ** -0.5
    hps = GLA_HEADS_PER_STEP
    assert heads % hps == 0

    def prepare(forward, h, q_ref, k_ref, v_ref, lr_ref, w2_ref, bg_ref):
        kcols, vcols = slice(h * dk, (h + 1) * dk), slice(h * dv, (h + 1) * dv)
        row = lax.broadcasted_iota(jnp.int32, (rt, rt), 0)
        col = lax.broadcasted_iota(jnp.int32, (rt, rt), 1)
        shift = ch.bit_length() - 1
        same = (row >> shift) == (col >> shift)
        if forward:
            cum_mask = same & (col <= row)
            score_mask = cum_mask
        else:
            cum_mask = same & (col >= row)
            score_mask = same & (col > row)
        gk = _log_sigmoid(_dot(lr_ref[...].astype(BF16), w2_ref[:, kcols]) + bg_ref[:, kcols]) / GLA_GATE_NORM
        tri = jnp.where(cum_mask, 1.0, 0.0).astype(BF16)
        g1, g2, g3 = _split3(gk)
        bcum = _dot(tri, g1) + _dot(tri, g2) + _dot(tri, g3)
        k = k_ref[:, kcols]
        v = v_ref[:, vcols].astype(BF16)
        q_s = (q_ref[:, kcols] * q_scale * jnp.exp(bcum)).astype(BF16)
        k_s = (k * jnp.exp(-bcum)).astype(BF16)
        scores = jnp.where(score_mask, _dot_nt(q_s, k_s), 0.0)
        o_intra = _dot(scores.astype(BF16), v)
        return q_s, k, v, bcum, o_intra

    def advance(forward, h, c, prep, o_ref, st_ref):
        q_s, k, v, bcum, o_intra = prep
        rows = slice(c * ch, (c + 1) * ch)
        last = c * ch + ch - 1 if forward else c * ch
        b_last = bcum[last:last + 1, :]
        state = st_ref[h]
        o_ref[rows, h * dv:(h + 1) * dv] = o_intra[rows] + _dot_nt(q_s[rows], state.astype(BF16))
        k_end = (k[rows] * jnp.exp(b_last - bcum[rows])).astype(BF16)
        st_ref[h] = state * jnp.exp(b_last) + _dot_tn(v[rows], k_end)

    def body(qf, kf, vf, lf, qb, kb, vb, lb, w2f, bgf, w2b, bgb, of, ob, stf, stb):
        @pl.when(pl.program_id(2) == 0)
        def _():
            stf[...] = jnp.zeros_like(stf)
            stb[...] = jnp.zeros_like(stb)

        prep_f = [prepare(True, h, qf, kf, vf, lf, w2f, bgf) for h in range(hps)]
        prep_b = [prepare(False, h, qb, kb, vb, lb, w2b, bgb) for h in range(hps)]
        for c in range(ncs):
            for h in range(hps):
                advance(True, h, c, prep_f[h], of, stf)
                advance(False, h, ncs - 1 - c, prep_b[h], ob, stb)

    def tok(forward):
        return (lambda b, h, n: b * nb + n) if forward else (lambda b, h, n: b * nb + nb - 1 - n)

    hg = heads // hps

    def specs(forward):
        t = tok(forward)
        return [pl.BlockSpec((rt, hps * dk), lambda b, h, n: (t(b, h, n), h)),
                pl.BlockSpec((rt, hps * dk), lambda b, h, n: (t(b, h, n), hg + h)),
                pl.BlockSpec((rt, hps * dv), lambda b, h, n: (t(b, h, n), hg + h)),
                pl.BlockSpec((rt, LANES), lambda b, h, n: (t(b, h, n), 0))]

    w2_spec = pl.BlockSpec((LANES, hps * dk), lambda b, h, n: (0, h))
    bg_spec = pl.BlockSpec((1, hps * dk), lambda b, h, n: (0, h))
    out = jax.ShapeDtypeStruct((bsz * seq, e), F32)
    tf, tbk = tok(True), tok(False)
    return pl.pallas_call(
        body,
        grid=(bsz, hg, nb),
        in_specs=specs(True) + specs(False) + [w2_spec, bg_spec, w2_spec, bg_spec],
        out_specs=(pl.BlockSpec((rt, hps * dv), lambda b, h, n: (tf(b, h, n), h)),
                   pl.BlockSpec((rt, hps * dv), lambda b, h, n: (tbk(b, h, n), h))),
        out_shape=(out, out),
        scratch_shapes=[pltpu.VMEM((hps, dv, dk), F32), pltpu.VMEM((hps, dv, dk), F32)],
        compiler_params=_params("parallel", "parallel", "arbitrary"),
        name="gla_core",
    )(proj, proj, proj, lowrank, proj, proj, proj, lowrank, w2_f, bg_f.reshape(1, -1),
      w2_b, bg_b.reshape(1, -1))


def _gla_gate(of, ob, proj, norm_w, tm=1024):
    m, e = of.shape
    heads = e // GLA_DV
    tm = _tile(m, tm)
    g_col0 = (proj.shape[1] - e) // GLA_DV

    def body(of_ref, ob_ref, g_ref, nw_ref, o_ref):
        o = of_ref[...] + ob_ref[...]
        o = o * lax.rsqrt(jnp.mean(o * o, axis=-1, keepdims=True) + EPS) * nw_ref[...]
        o_ref[...] = (o * _silu(g_ref[...])).astype(BF16)

    tile = pl.BlockSpec((tm, GLA_DV), lambda i, h: (i, h))
    return pl.pallas_call(
        body, grid=(m // tm, heads),
        in_specs=[tile, tile, pl.BlockSpec((tm, GLA_DV), lambda i, h: (i, g_col0 + h)),
                  pl.BlockSpec((1, GLA_DV), lambda i, h: (0, 0))],
        out_specs=tile, out_shape=jax.ShapeDtypeStruct((m, e), BF16),
        compiler_params=_params("parallel", "parallel"), name="gla_gate")(of, ob, proj, norm_w.reshape(1, GLA_DV))


def _gla_layer(x2, h, layer, bsz, seq, gate, w_in, w1_f, w2_f, bg_f, w1_b, w2_b, bg_b, gla_norm_w, w_out):
    e = w_out.shape[1]
    d = w_in.shape[1]
    rank = w1_f.shape[2]
    proj = _matmul("gla_in", h, w_in, layer, 0, w_in.shape[2], [], _no_specs, _identity, F32, seq)
    w1 = jnp.zeros((1, d, LANES), F32).at[0, :, :rank].set(w1_f[layer].astype(F32))
    w1 = w1.at[0, :, rank:2 * rank].set(w1_b[layer].astype(F32))
    lowrank = _matmul("gla_in_gate_rank", h, w1, 0, 0, LANES, [], _no_specs, _identity, F32, seq)
    kw = w2_f.shape[2]
    w2f = jnp.zeros((LANES, kw), BF16).at[:rank].set(w2_f[layer].astype(BF16))
    w2b = jnp.zeros((LANES, kw), BF16).at[rank:2 * rank].set(w2_b[layer].astype(BF16))
    of, ob = _gla_core(proj, lowrank, w2f, bg_f[layer].astype(F32), w2b, bg_b[layer].astype(F32), bsz, seq, e)
    a = _gla_gate(of, ob, proj, gla_norm_w[layer].astype(F32))
    return _out_proj("gla_out", a, w_out, layer, x2, gate, seq)


def _final_norm(x2, w, tm=512):
    m, d = x2.shape
    tm = _tile(m, tm)

    def body(x_ref, w_ref, o_ref):
        xv = x_ref[...]
        o_ref[...] = xv * lax.rsqrt(jnp.mean(xv * xv, axis=-1, keepdims=True) + EPS) * w_ref[...]

    return pl.pallas_call(
        body, grid=(m // tm,),
        in_specs=[pl.BlockSpec((tm, d), lambda i: (i, 0)), pl.BlockSpec((1, d), lambda i: (0, 0))],
        out_specs=pl.BlockSpec((tm, d), lambda i: (i, 0)),
        out_shape=jax.ShapeDtypeStruct((m, d), F32),
        compiler_params=_params("parallel"),
        name="final_norm",
    )(x2, w.reshape(1, d).astype(F32))


def kernel(x, c, mod_w, mod_b, norm_w, na_w_in, na_rpb, na_w_out, s5_w_in, s5_b_re, s5_b_im, s5_c_re, s5_c_im, s5_d, s5_a_re_fwd, s5_a_im_fwd, s5_a_re_bwd, s5_a_im_bwd, s5_log_step_fwd, s5_log_step_bwd, s5_w_glu, s5_b_glu, s5_w_out, gla_w_in, gla_gk_w1_fwd, gla_gk_w2_fwd, gla_gk_b_fwd, gla_gk_w1_bwd, gla_gk_w2_bwd, gla_gk_b_bwd, gla_norm_w, gla_w_out, final_norm_w):
    bsz, seq, d = x.shape
    depth = mod_w.shape[0]
    mod = _modulation(c.astype(F32), mod_w, mod_b.astype(F32))
    x2 = x.astype(F32).reshape(bsz * seq, d)
    for i in range(depth):
        kind, j = i % 3, i // 3
        shift, scale, gate = mod[i, :, :d], mod[i, :, d:2 * d], mod[i, :, 2 * d:]
        h = _norm_modulate(x2, seq, norm_w[i].astype(F32), scale, shift)
        if kind == 0:
            x2 = _na_layer(x2, h, j, bsz, seq, gate, na_w_in, na_rpb, na_w_out)
        elif kind == 1:
            x2 = _s5_layer(x2, h, j, bsz, seq, gate, s5_w_in, s5_b_re, s5_b_im, s5_c_re, s5_c_im, s5_d,
                           s5_a_re_fwd, s5_a_im_fwd, s5_a_re_bwd, s5_a_im_bwd, s5_log_step_fwd,
                           s5_log_step_bwd, s5_w_glu, s5_b_glu, s5_w_out)
        else:
            x2 = _gla_layer(x2, h, j, bsz, seq, gate, gla_w_in, gla_gk_w1_fwd, gla_gk_w2_fwd, gla_gk_b_fwd,
                            gla_gk_w1_bwd, gla_gk_w2_bwd, gla_gk_b_bwd, gla_norm_w, gla_w_out)
    return _final_norm(x2, final_norm_w).reshape(bsz, seq, d)
```

```python
import jax
import jax.numpy as jnp
from jax import lax
from jax.experimental import pallas as pl
from jax.experimental.pallas import tpu as pltpu

EPS = 1e-6
GRID_W = 64
NA_HEAD_DIM = 128
NA_ROWS = 8
NA_COLS = 16
NA_ROW_BLOCK = 8
NA_HEADS_PER_STEP = 2
NA_PIPELINE_GROUP = 8
S5_GROUP = 16
S5_STATE = 64
GLA_DV = 512
GLA_DK = 256
GLA_GATE_NORM = 16.0
GLA_CHUNK = 64
GLA_HEADS_PER_STEP = 4
NEG_INF = -1e30
LOG2_E = 1.4426950408889634

V7X_VMEM_BYTES = 64 * 1024 * 1024
VMEM_LIMIT = V7X_VMEM_BYTES - 8 * 1024 * 1024
LANES = 128
SUBLANES = 8
BF16 = jnp.bfloat16
F32 = jnp.float32


def _params(*sem):
    return pltpu.CompilerParams(dimension_semantics=sem, vmem_limit_bytes=VMEM_LIMIT)


def _silu(x):
    return x * jax.nn.sigmoid(x)


def _log_sigmoid(x):
    return jnp.minimum(x, 0.0) - jnp.log(1.0 + jnp.exp(-jnp.abs(x)))


def _dot(a, b):
    return jnp.dot(a, b, preferred_element_type=F32)


def _dot_nt(a, b):
    return lax.dot_general(a, b, (((1,), (1,)), ((), ())), preferred_element_type=F32)


def _dot_tn(a, b):
    return lax.dot_general(a, b, (((0,), (0,)), ((), ())), preferred_element_type=F32)


def _split3(x):
    p1 = x.astype(BF16)
    r1 = x - p1.astype(F32)
    p2 = r1.astype(BF16)
    p3 = (r1 - p2.astype(F32)).astype(BF16)
    return p1, p2, p3


def _dot_nt_x3(a, b):
    a1, a2, _ = _split3(a)
    b1, b2, _ = _split3(b)
    return _dot_nt(a1, b1) + (_dot_nt(a1, b2) + _dot_nt(a2, b1))


def _tile(n, want):
    t = min(want, n)
    assert n % t == 0
    return t


def _matmul(name, a, w, layer, col0, n, e_ins, e_specs, epilogue, out_dtype, seq, tm=1024, tn=512):
    m, k = a.shape
    tm = _tile(seq, tm)
    tn = _tile(n, tn)
    assert col0 % tn == 0
    jb = col0 // tn
    n_e = len(e_ins)

    def body(a_ref, w_ref, *rest):
        e_refs, o_ref, wb_ref = rest[:n_e], rest[n_e], rest[n_e + 1]

        @pl.when(pl.program_id(1) == 0)
        def _():
            wb_ref[...] = w_ref[...].astype(BF16)

        o_ref[...] = epilogue(_dot(a_ref[...], wb_ref[...]), *e_refs).astype(out_dtype)

    return pl.pallas_call(
        body,
        grid=(n // tn, m // tm),
        in_specs=[pl.BlockSpec((tm, k), lambda j, i: (i, 0)),
                  pl.BlockSpec((None, k, tn), lambda j, i: (layer, 0, jb + j))] + list(e_specs(tm, tn)),
        out_specs=pl.BlockSpec((tm, tn), lambda j, i: (i, j)),
        out_shape=jax.ShapeDtypeStruct((m, n), out_dtype),
        scratch_shapes=[pltpu.VMEM((k, tn), BF16)],
        compiler_params=_params("parallel", "arbitrary"),
        name=name,
    )(a, w, *e_ins)


def _no_specs(tm, tn):
    return []


def _identity(acc):
    return acc


def _out_proj(name, a, w, layer, x2, gate, seq):
    bsz, d = gate.shape

    def specs(tm, tn):
        tiles_per_seq = seq // tm
        return [pl.BlockSpec((tm, tn), lambda j, i: (i, j)),
                pl.BlockSpec((1, 1, tn), lambda j, i: (i // tiles_per_seq, 0, j))]

    def epilogue(acc, x_ref, gate_ref):
        return x_ref[...] + gate_ref[0] * acc

    return _matmul(name, a, w, layer, 0, d, [x2, gate.reshape(bsz, 1, d)], specs, epilogue, F32, seq)


def _modulation(c, mod_w, mod_b):
    depth, d, n = mod_w.shape
    bsz = c.shape[0]
    c_pad = jnp.zeros((SUBLANES, d), F32).at[:bsz].set(c)
    tn = _tile(n, 512)

    def body(c_ref, w_ref, b_ref, o_ref):
        act = _silu(c_ref[...]).astype(BF16)
        o_ref[0] = _dot(act, w_ref[0].astype(BF16)) + b_ref[0]

    out = pl.pallas_call(
        body,
        grid=(depth, n // tn),
        in_specs=[pl.BlockSpec((SUBLANES, d), lambda i, j: (0, 0)),
                  pl.BlockSpec((1, d, tn), lambda i, j: (i, 0, j)),
                  pl.BlockSpec((1, 1, tn), lambda i, j: (i, 0, j))],
        out_specs=pl.BlockSpec((1, SUBLANES, tn), lambda i, j: (i, 0, j)),
        out_shape=jax.ShapeDtypeStruct((depth, SUBLANES, n), F32),
        compiler_params=_params("parallel", "parallel"),
        name="modulation",
    )(c_pad, mod_w, mod_b.reshape(depth, 1, n))
    return out[:, :bsz]


def _norm_modulate(x2, seq, norm_w, scale, shift, tm=512):
    m, d = x2.shape
    tm = _tile(seq, tm)
    tiles_per_seq = seq // tm
    bsz = m // seq

    def body(x_ref, nw_ref, sc_ref, sh_ref, o_ref):
        xv = x_ref[...]
        y = xv * lax.rsqrt(jnp.mean(xv * xv, axis=-1, keepdims=True) + EPS) * nw_ref[...]
        o_ref[...] = (y * (1.0 + sc_ref[0]) + sh_ref[0]).astype(BF16)

    vec_spec = pl.BlockSpec((1, 1, d), lambda i: (i // tiles_per_seq, 0, 0))
    return pl.pallas_call(
        body, grid=(m // tm,),
        in_specs=[pl.BlockSpec((tm, d), lambda i: (i, 0)), pl.BlockSpec((1, d), lambda i: (0, 0)),
                  vec_spec, vec_spec],
        out_specs=pl.BlockSpec((tm, d), lambda i: (i, 0)),
        out_shape=jax.ShapeDtypeStruct((m, d), BF16),
        compiler_params=_params("parallel"),
        name="norm_modulate",
    )(x2, norm_w.reshape(1, d), scale.reshape(bsz, 1, d), shift.reshape(bsz, 1, d))


def _na_bias_table(rpb):
    w = GRID_W
    qc = jnp.arange(w)
    kc = jnp.arange(w)
    col_start = jnp.clip(qc - NA_COLS // 2, 0, w - NA_COLS)
    col_ok = (kc[None, :] >= col_start[:, None]) & (kc[None, :] < col_start[:, None] + NA_COLS)
    dc = kc[None, :] - qc[:, None] + NA_COLS - 1
    onehot = ((dc[None] == jnp.arange(2 * NA_COLS - 1)[:, None, None]) & col_ok[None]).astype(F32)
    cexp = jnp.einsum('hdc,cqk->hdqk', rpb.astype(F32), onehot, precision=lax.Precision.HIGHEST)
    cexp = jnp.where(col_ok[None, None], cexp * LOG2_E, NEG_INF)
    tables = [jnp.concatenate([cexp[:, t + m] for m in range(NA_ROWS)], axis=-1) for t in range(NA_ROWS)]
    return jnp.stack(tables, axis=1)


def _na_attention(qkv, g, bias, bsz, seq):
    e = g.shape[1]
    heads = e // NA_HEAD_DIM
    rows = seq // GRID_W
    tq = NA_ROW_BLOCK * GRID_W
    nblk = rows // NA_ROW_BLOCK
    assert nblk >= 2 and rows % NA_ROW_BLOCK == 0
    half = tq // 2
    hp = NA_HEADS_PER_STEP
    hw = hp * NA_HEAD_DIM
    win = NA_ROWS * GRID_W
    half_rows = NA_ROWS // 2
    scale = NA_HEAD_DIM ** -0.5 * LOG2_E
    assert heads % hp == 0 and NA_ROW_BLOCK == NA_ROWS

    def body(q_ref, kp_ref, kc_ref, kn_ref, vp_ref, vc_ref, vn_ref, g_ref, b_ref, o_ref):
        def window(above_ref, cur_ref, below_ref, keys, lanes):
            parts = []
            for ref, lo, hi in ((above_ref, 0, half), (cur_ref, half, half + tq), (below_ref, half + tq, 2 * tq)):
                a, b = max(keys.start, lo), min(keys.stop, hi)
                if a < b:
                    parts.append(ref[a - lo:b - lo, lanes])
            return parts[0] if len(parts) == 1 else jnp.concatenate(parts, axis=0)

        def attend(r0):
            def place(qr, h):
                r = r0 + qr
                row_start = min(max(r - half_rows, 0), rows - NA_ROWS)
                start = (row_start - r0 + half_rows) * GRID_W
                return (slice(qr * GRID_W, (qr + 1) * GRID_W), slice(h * NA_HEAD_DIM, (h + 1) * NA_HEAD_DIM),
                        slice(start, start + win), row_start - r + NA_ROWS - 1)

            def scores(qr, h):
                q_rows, lanes, keys, table = place(qr, h)
                k = window(kp_ref, kc_ref, kn_ref, keys, lanes)
                return _dot_nt(q_ref[q_rows, lanes], k) * scale + b_ref[h, table]

            def finish(qr, h, s):
                q_rows, lanes, keys, _ = place(qr, h)
                p = jnp.exp2(s - jnp.max(s, axis=-1, keepdims=True))
                denom = jnp.sum(p, axis=-1, keepdims=True)
                o = _dot(p.astype(BF16), window(vp_ref, vc_ref, vn_ref, keys, lanes)) / denom
                o_ref[q_rows, lanes] = (o * _silu(g_ref[q_rows, lanes])).astype(BF16)

            work = [(qr, h) for qr in range(NA_ROW_BLOCK) for h in range(hp)]
            groups = [work[i:i + NA_PIPELINE_GROUP] for i in range(0, len(work), NA_PIPELINE_GROUP)]
            pending = [scores(*w) for w in groups[0]]
            for gi, grp in enumerate(groups):
                ahead = [scores(*w) for w in groups[gi + 1]] if gi + 1 < len(groups) else []
                for w, s in zip(grp, pending):
                    finish(*w, s)
                pending = ahead

        blk = pl.program_id(2)
        pl.when(blk == 0)(lambda: attend(0))
        pl.when(blk == nblk - 1)(lambda: attend(rows - NA_ROW_BLOCK))
        pl.when((blk > 0) & (blk < nblk - 1))(lambda: attend(NA_ROW_BLOCK))

    def cur(col0):
        return pl.BlockSpec((tq, hw), lambda b, h, i: (b * nblk + i, col0 + h))

    def above(col0):
        return pl.BlockSpec((half, hw), lambda b, h, i: (jnp.maximum(2 * (b * nblk + i) - 1, 0), col0 + h))

    def below(col0):
        last = 2 * bsz * nblk - 1
        return pl.BlockSpec((half, hw), lambda b, h, i: (jnp.minimum(2 * (b * nblk + i) + 2, last), col0 + h))

    kcol, vcol = heads // hp, 2 * heads // hp
    return pl.pallas_call(
        body,
        grid=(bsz, heads // hp, nblk),
        in_specs=[cur(0),
                  above(kcol), cur(kcol), below(kcol),
                  above(vcol), cur(vcol), below(vcol),
                  cur(0),
                  pl.BlockSpec((hp, NA_ROWS, GRID_W, win), lambda b, h, i: (h, 0, 0, 0))],
        out_specs=cur(0),
        out_shape=jax.ShapeDtypeStruct((bsz * seq, e), BF16),
        compiler_params=_params("parallel", "parallel", "arbitrary"),
        name="na_attention",
    )(qkv, qkv, qkv, qkv, qkv, qkv, qkv, g, bias)


def _na_layer(x2, h, layer, bsz, seq, gate, w_in, rpb, w_out):
    e = w_out.shape[1]
    qkv = _matmul("na_in_qkv", h, w_in, layer, 0, 3 * e, [], _no_specs, _identity, BF16, seq)
    g = _matmul("na_in_gate", h, w_in, layer, 3 * e, e, [], _no_specs, _identity, F32, seq)
    a = _na_attention(qkv, g, _na_bias_table(rpb[layer]), bsz, seq)
    return _out_proj("na_out", a, w_out, layer, x2, gate, seq)


S5_CHUNK = 64


def _s5_powers(a_re, a_im, log_step, b_re_t, b_im_t, n_pow):
    n_in, rows, lanes = b_re_t.shape

    def body(are_ref, aim_ref, ls_ref, bre_ref, bim_ref, pre_ref, pim_ref, bbre_ref, bbim_ref):
        dt = jnp.exp(ls_ref[...])
        are, aim = are_ref[...], aim_ref[...]
        mag = jnp.exp(are * dt)
        lam_re = mag * jnp.cos(aim * dt)
        lam_im = mag * jnp.sin(aim * dt)
        den = are * are + aim * aim
        nr = lam_re - 1.0
        r_re = (nr * are + lam_im * aim) / den
        r_im = (lam_im * are - nr * aim) / den
        for i in range(n_in):
            bbre_ref[i] = r_re * bre_ref[i] - r_im * bim_ref[i]
            bbim_ref[i] = r_re * bim_ref[i] + r_im * bre_ref[i]

        def power(tau, carry):
            t = jnp.asarray(tau, F32)
            dtv = jnp.exp(ls_ref[...])
            m = jnp.exp(t * (are_ref[...] * dtv))
            ang = t * (aim_ref[...] * dtv)
            pre_ref[tau] = m * jnp.cos(ang)
            pim_ref[tau] = m * jnp.sin(ang)
            return carry

        lax.fori_loop(0, n_pow, power, 0)

    pw = jax.ShapeDtypeStruct((n_pow, rows, lanes), F32)
    bb = jax.ShapeDtypeStruct((n_in, rows, lanes), F32)
    return pl.pallas_call(body, out_shape=(pw, pw, bb, bb), name="s5_powers",
                          compiler_params=pltpu.CompilerParams(vmem_limit_bytes=VMEM_LIMIT),
                          )(a_re, a_im, log_step, b_re_t, b_im_t)


def _s5_tables(lr_in, li_in, xa, xb, lr_out, li_out, ca, cb,
               lhs_b, lhs_f, br2_f, bi2_f, br2_b, bi2_b, ca2, cb2):
    groups, tc, width = lr_in.shape
    n_in = xa.shape[1]

    def body(lri, lii, xa_r, xb_r, lro, lio, ca_r, cb_r, lb, lf, brf, bif, brb, bib, ca2_r, cb2_r,
             pin_ref, pout_ref, dtab_ref):
        for s in range(tc):
            rows = slice(s * n_in, (s + 1) * n_in)
            pin_ref[0, rows, :] = (lri[0, s:s + 1, :] * xa_r[0] + lii[0, s:s + 1, :] * xb_r[0]).astype(BF16)
            pout_ref[0, rows, :] = (lro[0, s:s + 1, :] * ca_r[0] + lio[0, s:s + 1, :] * cb_r[0]).astype(BF16)

        def gcat(br, bi):
            return jnp.concatenate(
                [br[0, j:j + 1, :] * ca2_r[0] + bi[0, j:j + 1, :] * cb2_r[0] for j in range(n_in)], axis=0)

        dtab_ref[0] = _dot_nt_x3(lb[0], gcat(brb, bib)) + _dot_nt_x3(lf[0], gcat(brf, bif))

    def spec(a):
        return pl.BlockSpec((1,) + a.shape[1:], lambda g: (g, 0, 0))

    ins = [lr_in, li_in, xa, xb, lr_out, li_out, ca, cb, lhs_b, lhs_f, br2_f, bi2_f, br2_b, bi2_b, ca2, cb2]
    pshape = jax.ShapeDtypeStruct((groups, tc * n_in, width), BF16)
    dshape = jax.ShapeDtypeStruct((groups, 2 * tc, n_in * n_in), F32)
    return pl.pallas_call(
        body, grid=(groups,), in_specs=[spec(a) for a in ins],
        out_specs=(spec(pshape), spec(pshape), spec(dshape)),
        out_shape=(pshape, pshape, dshape),
        compiler_params=_params("parallel"), name="s5_tables")(*ins)


def _s5_pack(u2, tc, rb=128):
    m, e = u2.shape
    r_total = m // tc
    rb = _tile(r_total, rb)
    gpl = LANES // S5_GROUP

    def body(x_ref, o_ref, xf_ref):
        xf_ref[...] = x_ref[...].astype(F32)
        for s in range(tc):
            slab_t = xf_ref[:, s, :].T
            for gl in range(gpl):
                o_ref[gl, s * S5_GROUP:(s + 1) * S5_GROUP, :] = (
                    slab_t[gl * S5_GROUP:(gl + 1) * S5_GROUP, :].astype(BF16))

    return pl.pallas_call(
        body, grid=(r_total // rb, e // LANES),
        in_specs=[pl.BlockSpec((rb, tc, LANES), lambda i, j: (i, 0, j))],
        out_specs=pl.BlockSpec((gpl, tc * S5_GROUP, rb), lambda i, j: (j, 0, i)),
        out_shape=jax.ShapeDtypeStruct((e // S5_GROUP, tc * S5_GROUP, r_total), BF16),
        scratch_shapes=[pltpu.VMEM((rb, tc, LANES), F32)],
        compiler_params=_params("parallel", "parallel"), name="s5_pack")(u2.reshape(r_total, tc, e))


def _s5_unpack(zt, tc, rb=128):
    groups, v, r_total = zt.shape
    e = groups * S5_GROUP
    rb = _tile(r_total, rb)
    gpl = LANES // S5_GROUP

    def body(z_ref, o_ref, of_ref):
        for t in range(tc):
            rows = slice(t * S5_GROUP, (t + 1) * S5_GROUP)
            slab_t = jnp.concatenate([z_ref[gl, rows, :].astype(F32) for gl in range(gpl)], axis=0)
            of_ref[:, t, :] = slab_t.T
        o_ref[...] = of_ref[...].astype(BF16)

    out = pl.pallas_call(
        body, grid=(r_total // rb, e // LANES),
        in_specs=[pl.BlockSpec((gpl, v, rb), lambda i, j: (j, 0, i))],
        out_specs=pl.BlockSpec((rb, tc, LANES), lambda i, j: (i, 0, j)),
        out_shape=jax.ShapeDtypeStruct((r_total, tc, e), BF16),
        scratch_shapes=[pltpu.VMEM((rb, tc, LANES), F32)],
        compiler_params=_params("parallel", "parallel"), name="s5_unpack")(zt)
    return out.reshape(r_total * tc, e)


def _s5_toeplitz(ut, drow, p_in, p_out_t, mu, skip, bsz):
    groups, v, rows = ut.shape
    n_in = drow.shape[1]
    tc = v // n_in
    per_tile = LANES // n_in
    n_tiles = tc // per_tile
    nch = rows // bsz
    w4 = p_in.shape[2]
    half = w4 // 2

    def body(u_ref, d_ref, pin_ref, pout_ref, mu_ref, skip_ref, z_ref, m_ref, sl_ref, spf_ref, spb_ref):
        taps = d_ref[0]
        for b in range(per_tile):
            off = (per_tile - 1 - b) * n_in
            shifted = taps if off == 0 else pltpu.roll(taps, taps.shape[1] - off, axis=1)
            shifted = shifted.astype(BF16)
            for a in range(n_tiles):
                s = a * per_tile + b
                lo = (n_tiles - 1 - a) * LANES
                m_ref[s * n_in:(s + 1) * n_in, :] = shifted[:, lo:lo + v]
        uv = u_ref[0]
        sl_ref[...] = _dot_tn(uv, pin_ref[0])
        mr, mi = mu_ref[0, 0:1, :], mu_ref[0, 1:2, :]

        def step(k, carry):
            out = []
            for bi in range(bsz):
                fr, fi, br, bim = carry[bi]
                rows_f = pl.ds(bi * nch + k * SUBLANES, SUBLANES)
                rows_b = pl.ds(bi * nch + nch - (k + 1) * SUBLANES, SUBLANES)
                loc_f, loc_b = sl_ref[rows_f, :], sl_ref[rows_b, :]
                for i in range(SUBLANES):
                    ib = SUBLANES - 1 - i
                    rf = bi * nch + k * SUBLANES + i
                    rb = bi * nch + nch - (k + 1) * SUBLANES + ib
                    spf_ref[rf:rf + 1, :half] = fr
                    spf_ref[rf:rf + 1, half:] = fi
                    spb_ref[rb:rb + 1, :half] = br
                    spb_ref[rb:rb + 1, half:] = bim
                    lfr, lfi = loc_f[i:i + 1, :half], loc_f[i:i + 1, half:]
                    lbr, lbi = loc_b[ib:ib + 1, :half], loc_b[ib:ib + 1, half:]
                    fr, fi = mr * fr - mi * fi + lfr, mr * fi + mi * fr + lfi
                    br, bim = mr * br - mi * bim + lbr, mr * bim + mi * br + lbi
                out.append((fr, fi, br, bim))
            return tuple(out)

        zero = jnp.zeros((1, half), F32)
        carry = tuple((zero, zero, zero, zero) for _ in range(bsz))
        for k in range(nch // SUBLANES):
            carry = step(k, carry)
        lane = lax.broadcasted_iota(jnp.int32, (rows, w4), 1)
        fwd_lane = (lane & (half - 1)) < (half // 2)
        carry_in = jnp.where(fwd_lane, spf_ref[...], spb_ref[...]).astype(BF16)
        y = _dot(m_ref[...], uv) + _dot_nt(pout_ref[0], carry_in)
        z_ref[0] = jax.nn.gelu(y + pltpu.repeat(skip_ref[0], tc, axis=0) * uv.astype(F32)).astype(BF16)

    def spec(a):
        nd = len(a.shape)
        return pl.BlockSpec((1,) + a.shape[1:], lambda g: (g,) + (0,) * (nd - 1))

    out = jax.ShapeDtypeStruct((groups, v, rows), BF16)
    return pl.pallas_call(
        body, grid=(groups,),
        in_specs=[spec(ut), spec(drow), spec(p_in), spec(p_out_t), spec(mu), spec(skip)],
        out_specs=spec(out), out_shape=out,
        scratch_shapes=[pltpu.VMEM((v, v), BF16), pltpu.VMEM((rows, w4), F32),
                        pltpu.VMEM((rows, w4), F32), pltpu.VMEM((rows, w4), F32)],
        compiler_params=_params("parallel"), name="s5_toeplitz")(ut, drow, p_in, p_out_t, mu, skip)


def _s5_mix(u2, bsz, seq, e, b_re, b_im, c_re, c_im, d_skip, a_re_f, a_im_f, a_re_b, a_im_b, ls_f, ls_b):
    groups, states, n_in = b_re.shape
    tc = S5_CHUNK
    nch = seq // tc
    flat = lambda t: t.astype(F32).reshape(groups * states // LANES, LANES)
    bt = lambda t: t.astype(F32).transpose(2, 0, 1).reshape(n_in, groups * states // LANES, LANES)
    cat = lambda *ts: jnp.concatenate(ts, axis=-1)
    flip = lambda t: jnp.flip(t, axis=1)

    def direction(a_re, a_im, ls):
        pr, pi, bbr, bbi = _s5_powers(flat(a_re), flat(a_im), flat(ls), bt(b_re), bt(b_im), tc + 1)
        per_group = lambda t: t.reshape(t.shape[0], groups, states).transpose(1, 0, 2)
        return per_group(pr), per_group(pi), per_group(bbr), per_group(bbi)

    pfr, pfi, bfr, bfi = direction(a_re_f, a_im_f, ls_f)
    pbr, pbi, bbr, bbi = direction(a_re_b, a_im_b, ls_b)
    cr, ci = c_re.astype(F32), c_im.astype(F32)

    lr_in = cat(flip(pfr[:, :tc]), pbr[:, :tc], flip(pfr[:, :tc]), pbr[:, :tc])
    li_in = cat(flip(pfi[:, :tc]), pbi[:, :tc], flip(pfi[:, :tc]), pbi[:, :tc])
    lr_out = cat(pfr[:, 1:], flip(pbr[:, 1:]), pfr[:, 1:], flip(pbr[:, 1:]))
    li_out = cat(pfi[:, 1:], flip(pbi[:, 1:]), pfi[:, 1:], flip(pbi[:, 1:]))
    xa, xb = cat(bfr, bbr, bfi, bbi), cat(-bfi, -bbi, bfr, bbr)
    ca, cb = cat(cr, cr, -ci, -ci), cat(-ci, -ci, -cr, -cr)
    lhs_b = jnp.pad(cat(flip(pbr[:, :tc]), flip(pbi[:, :tc])), ((0, 0), (0, tc), (0, 0)))
    lhs_f = jnp.pad(cat(pfr[:, :tc], pfi[:, :tc]), ((0, 0), (tc - 1, 1), (0, 0)))
    ca2, cb2 = cat(cr, -ci), cat(-ci, -cr)
    p_in, p_out_t, dtab = _s5_tables(lr_in, li_in, xa, xb, lr_out, li_out, ca, cb, lhs_b, lhs_f,
                                     cat(bfr, bfr), cat(bfi, bfi), cat(bbr, bbr), cat(bbi, bbi), ca2, cb2)
    taps = jnp.flip(dtab.reshape(groups, 2 * tc, n_in, n_in)[:, :2 * tc - 1], axis=1)
    taps = jnp.pad(taps, ((0, 0), (0, 1), (0, 0), (0, 0)))
    drow = taps.transpose(0, 3, 1, 2).reshape(groups, n_in, 2 * tc * n_in)
    mu = jnp.stack([cat(pfr[:, tc], pbr[:, tc]), cat(pfi[:, tc], pbi[:, tc])], axis=1)

    rows = bsz * nch
    skip = jnp.broadcast_to(d_skip.astype(F32)[:, :, None], (groups, n_in, rows))
    zt = _s5_toeplitz(_s5_pack(u2, tc), drow, p_in, p_out_t, mu, skip, bsz)
    return _s5_unpack(zt, tc)


def _s5_glu(z, g, w_glu, layer, b_glu, seq):
    e = z.shape[1]

    def specs(tm, tn):
        tile = pl.BlockSpec((tm, tn), lambda j, i: (i, j))
        return [tile, tile, pl.BlockSpec((1, tn), lambda j, i: (0, j))]

    def epilogue(acc, z_ref, g_ref, b_ref):
        return z_ref[...].astype(F32) * jax.nn.sigmoid(acc + b_ref[...]) * _silu(g_ref[...])

    return _matmul("s5_glu", z, w_glu, layer, 0, e, [z, g, b_glu.reshape(1, e)], specs, epilogue, BF16, seq)


def _s5_layer(x2, h, layer, bsz, seq, gate, w_in, b_re, b_im, c_re, c_im, d_skip,
              a_re_f, a_im_f, a_re_b, a_im_b, log_step_f, log_step_b, w_glu, b_glu, w_out):
    e = w_out.shape[1]
    u2 = _matmul("s5_in_u", h, w_in, layer, 0, e, [], _no_specs, _identity, BF16, seq)
    g = _matmul("s5_in_gate", h, w_in, layer, e, e, [], _no_specs, _identity, F32, seq)
    pick = lambda t: t[layer]
    z = _s5_mix(u2, bsz, seq, e, pick(b_re), pick(b_im), pick(c_re), pick(c_im), pick(d_skip), pick(a_re_f),
                pick(a_im_f), pick(a_re_b), pick(a_im_b), pick(log_step_f), pick(log_step_b))
    a = _s5_glu(z, g, w_glu, layer, b_glu[layer].astype(F32), seq)
    return _out_proj("s5_out", a, w_out, layer, x2, gate, seq)


def _gla_core(proj, lowrank, w2_f, bg_f, w2_b, bg_b, bsz, seq, e, chunks_per_step=4):
    heads = e // GLA_DV
    dk, dv, ch = GLA_DK, GLA_DV, GLA_CHUNK
    rt = min(chunks_per_step * ch, seq)
    ncs = rt // ch
    nb = seq // rt
    q_scale = dk ** -0.5
    hps = GLA_HEADS_PER_STEP
    assert heads % hps == 0

    def prepare(forward, h, q_ref, k_ref, v_ref, lr_ref, w2_ref, bg_ref):
        kcols, vcols = slice(h * dk, (h + 1) * dk), slice(h * dv, (h + 1) * dv)
        row = lax.broadcasted_iota(jnp.int32, (rt, rt), 0)
        col = lax.broadcasted_iota(jnp.int32, (rt, rt), 1)
        shift = ch.bit_length() - 1
        same = (row >> shift) == (col >> shift)
        if forward:
            cum_mask = same & (col <= row)
            score_mask = cum_mask
        else:
            cum_mask = same & (col >= row)
            score_mask = same & (col > row)
        gk = _log_sigmoid(_dot(lr_ref[...].astype(BF16), w2_ref[:, kcols]) + bg_ref[:, kcols]) / GLA_GATE_NORM
        tri = jnp.where(cum_mask, 1.0, 0.0).astype(BF16)
        g1, g2, g3 = _split3(gk)
        bcum = _dot(tri, g1) + _dot(tri, g2) + _dot(tri, g3)
        k = k_ref[:, kcols]
        v = v_ref[:, vcols].astype(BF16)
        q_s = (q_ref[:, kcols] * q_scale * jnp.exp(bcum)).astype(BF16)
        k_s = (k * jnp.exp(-bcum)).astype(BF16)
        scores = jnp.where(score_mask, _dot_nt(q_s, k_s), 0.0)
        o_intra = _dot(scores.astype(BF16), v)
        return q_s, k, v, bcum, o_intra

    def advance(forward, h, c, prep, o_ref, st_ref):
        q_s, k, v, bcum, o_intra = prep
        rows = slice(c * ch, (c + 1) * ch)
        last = c * ch + ch - 1 if forward else c * ch
        b_last = bcum[last:last + 1, :]
        state = st_ref[h]
        o_ref[rows, h * dv:(h + 1) * dv] = o_intra[rows] + _dot_nt(q_s[rows], state.astype(BF16))
        k_end = (k[rows] * jnp.exp(b_last - bcum[rows])).astype(BF16)
        st_ref[h] = state * jnp.exp(b_last) + _dot_tn(v[rows], k_end)

    def body(qf, kf, vf, lf, qb, kb, vb, lb, w2f, bgf, w2b, bgb, of, ob, stf, stb):
        @pl.when(pl.program_id(2) == 0)
        def _():
            stf[...] = jnp.zeros_like(stf)
            stb[...] = jnp.zeros_like(stb)

        prep_f = [prepare(True, h, qf, kf, vf, lf, w2f, bgf) for h in range(hps)]
        prep_b = [prepare(False, h, qb, kb, vb, lb, w2b, bgb) for h in range(hps)]
        for c in range(ncs):
            for h in range(hps):
                advance(True, h, c, prep_f[h], of, stf)
                advance(False, h, ncs - 1 - c, prep_b[h], ob, stb)

    def tok(forward):
        return (lambda b, h, n: b * nb + n) if forward else (lambda b, h, n: b * nb + nb - 1 - n)

    hg = heads // hps

    def specs(forward):
        t = tok(forward)
        return [pl.BlockSpec((rt, hps * dk), lambda b, h, n: (t(b, h, n), h)),
                pl.BlockSpec((rt, hps * dk), lambda b, h, n: (t(b, h, n), hg + h)),
                pl.BlockSpec((rt, hps * dv), lambda b, h, n: (t(b, h, n), hg + h)),
                pl.BlockSpec((rt, LANES), lambda b, h, n: (t(b, h, n), 0))]

    w2_spec = pl.BlockSpec((LANES, hps * dk), lambda b, h, n: (0, h))
    bg_spec = pl.BlockSpec((1, hps * dk), lambda b, h, n: (0, h))
    out = jax.ShapeDtypeStruct((bsz * seq, e), F32)
    tf, tbk = tok(True), tok(False)
    return pl.pallas_call(
        body,
        grid=(bsz, hg, nb),
        in_specs=specs(True) + specs(False) + [w2_spec, bg_spec, w2_spec, bg_spec],
        out_specs=(pl.BlockSpec((rt, hps * dv), lambda b, h, n: (tf(b, h, n), h)),
                   pl.BlockSpec((rt, hps * dv), lambda b, h, n: (tbk(b, h, n), h))),
        out_shape=(out, out),
        scratch_shapes=[pltpu.VMEM((hps, dv, dk), F32), pltpu.VMEM((hps, dv, dk), F32)],
        compiler_params=_params("parallel", "parallel", "arbitrary"),
        name="gla_core",
    )(proj, proj, proj, lowrank, proj, proj, proj, lowrank, w2_f, bg_f.reshape(1, -1),
      w2_b, bg_b.reshape(1, -1))


def _gla_gate(of, ob, proj, norm_w, tm=1024):
    m, e = of.shape
    heads = e // GLA_DV
    tm = _tile(m, tm)
    g_col0 = (proj.shape[1] - e) // GLA_DV

    def body(of_ref, ob_ref, g_ref, nw_ref, o_ref):
        o = of_ref[...] + ob_ref[...]
        o = o * lax.rsqrt(jnp.mean(o * o, axis=-1, keepdims=True) + EPS) * nw_ref[...]
        o_ref[...] = (o * _silu(g_ref[...])).astype(BF16)

    tile = pl.BlockSpec((tm, GLA_DV), lambda i, h: (i, h))
    return pl.pallas_call(
        body, grid=(m // tm, heads),
        in_specs=[tile, tile, pl.BlockSpec((tm, GLA_DV), lambda i, h: (i, g_col0 + h)),
                  pl.BlockSpec((1, GLA_DV), lambda i, h: (0, 0))],
        out_specs=tile, out_shape=jax.ShapeDtypeStruct((m, e), BF16),
        compiler_params=_params("parallel", "parallel"), name="gla_gate")(of, ob, proj, norm_w.reshape(1, GLA_DV))


def _gla_layer(x2, h, layer, bsz, seq, gate, w_in, w1_f, w2_f, bg_f, w1_b, w2_b, bg_b, gla_norm_w, w_out):
    e = w_out.shape[1]
    d = w_in.shape[1]
    rank = w1_f.shape[2]
    proj = _matmul("gla_in", h, w_in, layer, 0, w_in.shape[2], [], _no_specs, _identity, F32, seq)
    w1 = jnp.zeros((1, d, LANES), F32).at[0, :, :rank].set(w1_f[layer].astype(F32))
    w1 = w1.at[0, :, rank:2 * rank].set(w1_b[layer].astype(F32))
    lowrank = _matmul("gla_in_gate_rank", h, w1, 0, 0, LANES, [], _no_specs, _identity, F32, seq)
    kw = w2_f.shape[2]
    w2f = jnp.zeros((LANES, kw), BF16).at[:rank].set(w2_f[layer].astype(BF16))
    w2b = jnp.zeros((LANES, kw), BF16).at[rank:2 * rank].set(w2_b[layer].astype(BF16))
    of, ob = _gla_core(proj, lowrank, w2f, bg_f[layer].astype(F32), w2b, bg_b[layer].astype(F32), bsz, seq, e)
    a = _gla_gate(of, ob, proj, gla_norm_w[layer].astype(F32))
    return _out_proj("gla_out", a, w_out, layer, x2, gate, seq)


def _final_norm(x2, w, tm=512):
    m, d = x2.shape
    tm = _tile(m, tm)

    def body(x_ref, w_ref, o_ref):
        xv = x_ref[...]
        o_ref[...] = xv * lax.rsqrt(jnp.mean(xv * xv, axis=-1, keepdims=True) + EPS) * w_ref[...]

    return pl.pallas_call(
        body, grid=(m // tm,),
        in_specs=[pl.BlockSpec((tm, d), lambda i: (i, 0)), pl.BlockSpec((1, d), lambda i: (0, 0))],
        out_specs=pl.BlockSpec((tm, d), lambda i: (i, 0)),
        out_shape=jax.ShapeDtypeStruct((m, d), F32),
        compiler_params=_params("parallel"),
        name="final_norm",
    )(x2, w.reshape(1, d).astype(F32))


def kernel(x, c, mod_w, mod_b, norm_w, na_w_in, na_rpb, na_w_out, s5_w_in, s5_b_re, s5_b_im, s5_c_re, s5_c_im, s5_d, s5_a_re_fwd, s5_a_im_fwd, s5_a_re_bwd, s5_a_im_bwd, s5_log_step_fwd, s5_log_step_bwd, s5_w_glu, s5_b_glu, s5_w_out, gla_w_in, gla_gk_w1_fwd, gla_gk_w2_fwd, gla_gk_b_fwd, gla_gk_w1_bwd, gla_gk_w2_bwd, gla_gk_b_bwd, gla_norm_w, gla_w_out, final_norm_w):
    bsz, seq, d = x.shape
    depth = mod_w.shape[0]
    mod = _modulation(c.astype(F32), mod_w, mod_b.astype(F32))
    x2 = x.astype(F32).reshape(bsz * seq, d)
    for i in range(depth):
        kind, j = i % 3, i // 3
        shift, scale, gate = mod[i, :, :d], mod[i, :, d:2 * d], mod[i, :, 2 * d:]
        h = _norm_modulate(x2, seq, norm_w[i].astype(F32), scale, shift)
        if kind == 0:
            x2 = _na_layer(x2, h, j, bsz, seq, gate, na_w_in, na_rpb, na_w_out)
        elif kind == 1:
            x2 = _s5_layer(x2, h, j, bsz, seq, gate, s5_w_in, s5_b_re, s5_b_im, s5_c_re, s5_c_im, s5_d,
                           s5_a_re_fwd, s5_a_im_fwd, s5_a_re_bwd, s5_a_im_bwd, s5_log_step_fwd,
                           s5_log_step_bwd, s5_w_glu, s5_b_glu, s5_w_out)
        else:
            x2 = _gla_layer(x2, h, j, bsz, seq, gate, gla_w_in, gla_gk_w1_fwd, gla_gk_w2_fwd, gla_gk_b_fwd,
                            gla_gk_w1_bwd, gla_gk_w2_bwd, gla_gk_b_bwd, gla_norm_w, gla_w_out)
    return _final_norm(x2, final_norm_w).reshape(bsz, seq, d)
```
